```python
import math
import jax, jax.numpy as jnp
from jax import lax
import numpy as np

D_MODEL = 1024
BATCH = 4
SEQ = 8192
DEPTH = 2
DEC_BATCH = 32
DEC_SEQ = 16
PAST_LEN = 1024

CHUNK = 64
MIX_WIDTH = D_MODEL
HG_WIDTH = MIX_WIDTH // 2
HG_HEAD_DIM = 128
HG_HEADS = HG_WIDTH // HG_HEAD_DIM
ATT_WIDTH = MIX_WIDTH - HG_WIDTH
ATT_HEAD_DIM = 64
ATT_HEADS = ATT_WIDTH // ATT_HEAD_DIM
LEFT_CHUNKS = 8
ATT_REACH = LEFT_CHUNKS * CHUNK
BAND = ATT_REACH + CHUNK
REL_CLIP = 128
N_REL = 2 * REL_CLIP + 1
D_FF = 4 * D_MODEL
IN_WIDTH = 4 * HG_WIDTH + 3 * ATT_WIDTH
SPLITS = [HG_WIDTH, 2 * HG_WIDTH, 3 * HG_WIDTH, 4 * HG_WIDTH,
          4 * HG_WIDTH + ATT_WIDTH, 4 * HG_WIDTH + 2 * ATT_WIDTH]
EPS = 1e-6
NEG = -1e30

kernel_name = 'hymba_hgrn2_chunkband_stream_step'


def rmsnorm(x, g):
    xf = x.astype(jnp.float32)
    y = xf * lax.rsqrt(jnp.mean(xf * xf, axis=-1, keepdims=True) + EPS)
    return (y * g.astype(jnp.float32)).astype(x.dtype)


def hgrn2_scan(q, k, v, logf, s0):
    B, T, H, DK = q.shape
    C = min(CHUNK, T)
    N = T // C

    def to_chunks(a):
        return a.reshape(B, N, C, H, a.shape[-1]).transpose(1, 0, 3, 2, 4)

    qc, kc, vc, gc = to_chunks(q), to_chunks(k), to_chunks(v), to_chunks(logf)
    causal = jnp.tril(jnp.ones((C, C), dtype=bool))[:, :, None]

    def step(S, inp):
        qb, kb, vb, gb = inp
        b = jnp.cumsum(gb, axis=2)
        inter = jnp.einsum('bhtk,bhkv->bhtv', qb * jnp.exp(b), S)
        diff = b[:, :, :, None, :] - b[:, :, None, :, :]
        decay = jnp.where(causal, jnp.exp(jnp.minimum(diff, 0.0)), 0.0)
        scores = jnp.einsum('bhtsk,bhsk->bhts', qb[:, :, :, None, :] * decay, kb)
        intra = jnp.einsum('bhts,bhsv->bhtv', scores, vb)
        b_last = b[:, :, -1:, :]
        S_new = (jnp.exp(b_last[:, :, 0, :])[..., None] * S
                 + jnp.einsum('bhsk,bhsv->bhkv', kb * jnp.exp(b_last - b), vb))
        return S_new, inter + intra

    S_fin, o = lax.scan(step, s0, (qc, kc, vc, gc))
    o = o.transpose(1, 0, 3, 2, 4).reshape(B, T, H, v.shape[-1])
    return o, S_fin


def band_attention(q, k, v, qpos, kpos, kvalid, rel_bias):
    rel = jnp.clip(kpos[None, :] - qpos[:, None], -REL_CLIP, REL_CLIP) + REL_CLIP
    bias = rel_bias.astype(jnp.float32)[:, rel]
    s = jnp.einsum('bqhd,bkhd->bhqk', q, k, preferred_element_type=jnp.float32)
    s = s * (1.0 / math.sqrt(ATT_HEAD_DIM)) + bias[None]
    s = jnp.where(kvalid[None, None, None, :], s, NEG)
    p = jax.nn.softmax(s, axis=-1)
    return jnp.einsum('bhqk,bkhd->bqhd', p.astype(v.dtype), v)


def prompt_attention(q, k, v, rel_bias):
    B, T, H, Dh = q.shape
    N = T // CHUNK
    pad = ((0, 0), (ATT_REACH, 0), (0, 0), (0, 0))
    kp = jnp.pad(k, pad)
    vp = jnp.pad(v, pad)
    qc = q.reshape(B, N, CHUNK, H, Dh).transpose(1, 0, 2, 3, 4)
    key_off = jnp.arange(BAND) - ATT_REACH
    q_off = jnp.arange(CHUNK)

    def one_chunk(args):
        c, qb = args
        start = c * CHUNK
        kb = lax.dynamic_slice_in_dim(kp, start, BAND, axis=1)
        vb = lax.dynamic_slice_in_dim(vp, start, BAND, axis=1)
        kpos = start + key_off
        return band_attention(qb, kb, vb, start + q_off, kpos, kpos >= 0, rel_bias)

    o = lax.map(one_chunk, (jnp.arange(N), qc))
    return o.transpose(1, 0, 2, 3, 4).reshape(B, T, H, Dh)


def trunk_layer(x, s0, k_past, v_past, lb, norm1_g, w_in, hg_norm_g, rel_bias,
                att_norm_g, w_out, norm2_g, w_up, w_down):
    B, T, _ = x.shape
    f32 = jnp.float32
    h = rmsnorm(x, norm1_g)
    proj = h @ w_in
    hq, hf, hi, hg, aq, ak, av = jnp.split(proj, SPLITS, axis=-1)

    fgate = lb + (1.0 - lb) * jax.nn.sigmoid(hf.astype(f32))
    logf = jnp.log(fgate)
    kk = 1.0 - fgate
    hd = lambda a: a.reshape(B, T, HG_HEADS, HG_HEAD_DIM)
    if s0 is None:
        s0 = jnp.zeros((B, HG_HEADS, HG_HEAD_DIM, HG_HEAD_DIM), f32)
    o_h, s_new = hgrn2_scan(hd(hq.astype(f32)), hd(kk), hd(hi.astype(f32)), hd(logf), s0.astype(f32))
    o_h = rmsnorm(o_h, hg_norm_g.reshape(HG_HEADS, HG_HEAD_DIM)).reshape(B, T, HG_WIDTH)
    o_h = (o_h * jax.nn.silu(hg.astype(f32))).astype(x.dtype)

    ad = lambda a: a.reshape(B, T, ATT_HEADS, ATT_HEAD_DIM)
    q, k, v = ad(aq), ad(ak), ad(av)
    if k_past is None:
        o_a = prompt_attention(q, k, v, rel_bias)
        keep = min(ATT_REACH, T)
        k_rows, v_rows = k[:, T - keep:], v[:, T - keep:]
    else:
        W = k_past.shape[1]
        k_all = jnp.concatenate([k_past.astype(k.dtype), k], axis=1)
        v_all = jnp.concatenate([v_past.astype(v.dtype), v], axis=1)
        kpos = jnp.arange(W + T) - W
        o_a = band_attention(q, k_all, v_all, jnp.arange(T), kpos,
                             jnp.ones((W + T,), dtype=bool), rel_bias)
        k_rows, v_rows = k, v
    o_a = rmsnorm(o_a.reshape(B, T, ATT_WIDTH), att_norm_g)

    x = x + jnp.concatenate([o_h, o_a.astype(x.dtype)], axis=-1) @ w_out
    u = rmsnorm(x, norm2_g) @ w_up
    x = x + jnp.square(jax.nn.relu(u)) @ w_down
    return x, s_new, k_rows, v_rows


def setup_inputs(seed: int = 0) -> dict:
    key = jax.random.key(seed)
    ks = jax.random.split(key, 16)
    n = lambda k, shape, s: (jax.random.normal(k, shape, jnp.float32) * s)
    att_cache_len = min(ATT_REACH, PAST_LEN)
    return {
        'x_prompt': n(ks[0], (BATCH, SEQ, D_MODEL), 1.0),
        'x_sample': n(ks[1], (DEC_BATCH, DEC_SEQ, D_MODEL), 1.0),
        'state_hgrn': n(ks[2], (DEPTH, DEC_BATCH, HG_HEADS, HG_HEAD_DIM, HG_HEAD_DIM), 0.5),
        'cache_k': n(ks[3], (DEPTH, DEC_BATCH, att_cache_len, ATT_HEADS, ATT_HEAD_DIM), 1.0),
        'cache_v': n(ks[4], (DEPTH, DEC_BATCH, att_cache_len, ATT_HEADS, ATT_HEAD_DIM), 1.0),
        'lb_param': n(ks[5], (DEPTH, HG_WIDTH), 0.5),
        'norm1_g': 1.0 + n(ks[6], (DEPTH, D_MODEL), 0.1),
        'w_in': n(ks[7], (DEPTH, D_MODEL, IN_WIDTH), D_MODEL ** -0.5),
        'hg_norm_g': 1.0 + n(ks[8], (DEPTH, HG_WIDTH), 0.1),
        'rel_bias': n(ks[9], (DEPTH, ATT_HEADS, N_REL), 0.5),
        'att_norm_g': 1.0 + n(ks[10], (DEPTH, ATT_WIDTH), 0.1),
        'w_out': n(ks[11], (DEPTH, MIX_WIDTH, D_MODEL), MIX_WIDTH ** -0.5),
        'norm2_g': 1.0 + n(ks[12], (DEPTH, D_MODEL), 0.1),
        'w_up': n(ks[13], (DEPTH, D_MODEL, D_FF), D_MODEL ** -0.5),
        'w_down': n(ks[14], (DEPTH, D_FF, D_MODEL), 0.5 * D_FF ** -0.5),
        'final_norm_g': 1.0 + n(ks[15], (D_MODEL,), 0.1),
    }


def reference(x_prompt, x_sample, state_hgrn, cache_k, cache_v, lb_param, norm1_g, w_in,
              hg_norm_g, rel_bias, att_norm_g, w_out, norm2_g, w_up, w_down, final_norm_g):
    sm = jax.nn.softmax(lb_param.astype(jnp.float32), axis=0)
    lbs = jnp.cumsum(sm, axis=0) - sm[0:1]
    xp, xs = x_prompt, x_sample
    sp_l, kp_l, vp_l, ss_l, ks_l, vs_l = [], [], [], [], [], []
    for l in range(DEPTH):
        lw = (lbs[l], norm1_g[l], w_in[l], hg_norm_g[l], rel_bias[l], att_norm_g[l],
              w_out[l], norm2_g[l], w_up[l], w_down[l])
        xp, sp, kp, vp = trunk_layer(xp, None, None, None, *lw)
        xs, ss, ksm, vsm = trunk_layer(xs, state_hgrn[l], cache_k[l], cache_v[l], *lw)
        sp_l.append(sp); kp_l.append(kp); vp_l.append(vp)
        ss_l.append(ss); ks_l.append(ksm); vs_l.append(vsm)
    y_prompt = rmsnorm(xp, final_norm_g)
    y_sample = rmsnorm(xs, final_norm_g)
    new_state_hgrn_prompt = jnp.stack(sp_l)
    new_cache_k_prompt = jnp.stack(kp_l)
    new_cache_v_prompt = jnp.stack(vp_l)
    new_state_hgrn_sample = jnp.stack(ss_l)
    new_cache_k_sample = jnp.stack(ks_l)
    new_cache_v_sample = jnp.stack(vs_l)
    return (y_prompt, y_sample, new_state_hgrn_prompt, new_cache_k_prompt, new_cache_v_prompt,
            new_state_hgrn_sample, new_cache_k_sample, new_cache_v_sample)
```

```python
import functools
import math

import jax
import jax.numpy as jnp
from jax import lax
from jax.experimental import pallas as pl
from jax.experimental.pallas import tpu as pltpu

F32 = jnp.float32
BF16 = jnp.bfloat16

D_MODEL = 1024
DEPTH = 2
CHUNK = 64
HG_WIDTH = 512
HG_HEAD_DIM = 128
HG_HEADS = 4
ATT_WIDTH = 512
ATT_HEAD_DIM = 64
ATT_HEADS = 8
LEFT_CHUNKS = 8
ATT_REACH = LEFT_CHUNKS * CHUNK
REL_CLIP = 128
D_FF = 4 * D_MODEL
IN_WIDTH = 4 * HG_WIDTH + 3 * ATT_WIDTH
EPS = 1e-6
NEG = -1e30

LANES = 128
ROW_TILE = 512
Q_BLOCK = 256
KEY_WINDOW = ATT_REACH + Q_BLOCK
HG_COLS = 5 * HG_WIDTH
SAFE_EXP = 80.0
VMEM_LIMIT = 60 * 1024 * 1024


def _resident(shape):
    nd = len(shape)
    return pl.BlockSpec(shape, lambda *_: (0,) * nd, pipeline_mode=pl.Buffered(1))


def _rms(x, g):
    ms = jnp.mean(x * x, axis=-1, keepdims=True)
    return x * lax.rsqrt(ms + EPS) * g


def _inproj_body(layer, x_ref, g_ref, w_ref, lbp_ref, hg_ref, aq_ref, ak_ref, av_ref,
                 kf_ref, vf_ref):
    h = _rms(x_ref[...], g_ref[...]).astype(BF16)

    def proj(sec):
        return jnp.dot(h, w_ref[:, sec * HG_WIDTH:(sec + 1) * HG_WIDTH],
                       preferred_element_type=F32)

    lp = lbp_ref[...]
    e = jnp.exp(lp - jnp.max(lp, axis=0, keepdims=True))
    sm = e / jnp.sum(e, axis=0, keepdims=True)
    lb = sm[0:1]
    for i in range(1, layer + 1):
        lb = lb + sm[i:i + 1]
    lb = lb - sm[0:1]

    f = lb + (1.0 - lb) * jax.nn.sigmoid(proj(1))
    hg_ref[:, 0 * HG_WIDTH:1 * HG_WIDTH] = proj(0)
    hg_ref[:, 1 * HG_WIDTH:2 * HG_WIDTH] = jnp.log(f)
    hg_ref[:, 2 * HG_WIDTH:3 * HG_WIDTH] = 1.0 - f
    hg_ref[:, 3 * HG_WIDTH:4 * HG_WIDTH] = proj(2)
    hg_ref[:, 4 * HG_WIDTH:5 * HG_WIDTH] = proj(3)
    aq_ref[...] = (proj(4) * (1.0 / math.sqrt(ATT_HEAD_DIM))).astype(BF16)
    k = proj(5)
    v = proj(6)
    ak_ref[...] = k.astype(BF16)
    av_ref[...] = v.astype(BF16)
    kf_ref[...] = k
    vf_ref[...] = v


def _inproj(layer, x, g, w, lbp):
    rows = x.shape[0]
    grid = (rows // ROW_TILE,)
    row_spec = lambda width: pl.BlockSpec((ROW_TILE, width), lambda i: (i, 0))
    return pl.pallas_call(
        functools.partial(_inproj_body, layer),
        grid=grid,
        in_specs=[row_spec(D_MODEL), _resident((1, D_MODEL)), _resident((D_MODEL, IN_WIDTH)),
                  _resident((DEPTH, HG_WIDTH))],
        out_specs=[row_spec(HG_COLS), row_spec(ATT_WIDTH), row_spec(ATT_WIDTH),
                   row_spec(ATT_WIDTH), row_spec(ATT_WIDTH), row_spec(ATT_WIDTH)],
        out_shape=[jax.ShapeDtypeStruct((rows, HG_COLS), F32),
                   jax.ShapeDtypeStruct((rows, ATT_WIDTH), BF16),
                   jax.ShapeDtypeStruct((rows, ATT_WIDTH), BF16),
                   jax.ShapeDtypeStruct((rows, ATT_WIDTH), BF16),
                   jax.ShapeDtypeStruct((rows, ATT_WIDTH), F32),
                   jax.ShapeDtypeStruct((rows, ATT_WIDTH), F32)],
        compiler_params=pltpu.CompilerParams(dimension_semantics=("parallel",),
                                             vmem_limit_bytes=VMEM_LIMIT),
        name="inproj",
    )(x, g, w, lbp)


def _split3(x):
    hi = x.astype(BF16)
    r1 = x - hi.astype(F32)
    mid = r1.astype(BF16)
    lo = (r1 - mid.astype(F32)).astype(BF16)
    return hi, mid, lo


def _hgrn_body(chunk, n_chunks, hg_ref, s0_ref, gn_ref, o_ref, sout_ref, st_ref, sstart_ref):
    t = pl.program_id(1)
    mid = chunk // 2

    @pl.when(t == 0)
    def _():
        for h in range(HG_HEADS):
            st_ref[h] = s0_ref[0, h].T

    row_i = lax.broadcasted_iota(jnp.int32, (chunk, chunk), 0)
    col_i = lax.broadcasted_iota(jnp.int32, (chunk, chunk), 1)
    causal = row_i >= col_i
    tri = jnp.where(causal, 1.0, 0.0).astype(BF16)

    def direct_scores(q, k, b):
        lane = lax.broadcasted_iota(jnp.int32, (chunk, LANES), 1)

        def body(s, sc):
            sel = lax.broadcasted_iota(jnp.int32, (chunk, 1), 0) == s
            bs = jnp.sum(jnp.where(sel, b, 0.0), axis=0, keepdims=True)
            ks = jnp.sum(jnp.where(sel, k, 0.0), axis=0, keepdims=True)
            col = jnp.sum(q * ks * jnp.exp(jnp.minimum(b - bs, 0.0)), axis=-1, keepdims=True)
            return jnp.where(lane == s, col, sc)

        sc = lax.fori_loop(0, chunk, body, jnp.zeros((chunk, LANES), F32))
        return sc[:, :chunk]

    def run(fast):
        def chunk_step(c, worst):
            rows = pl.ds(pl.multiple_of(c * chunk, chunk), chunk)
            g_all = hg_ref[rows, 1 * HG_WIDTH:2 * HG_WIDTH]
            b_all = sum(jnp.dot(tri, part, preferred_element_type=F32) for part in _split3(g_all))
            for h in range(HG_HEADS):
                lanes = slice(h * HG_HEAD_DIM, (h + 1) * HG_HEAD_DIM)
                col = lambda sec: slice(sec * HG_WIDTH + h * HG_HEAD_DIM,
                                        sec * HG_WIDTH + (h + 1) * HG_HEAD_DIM)
                q = hg_ref[rows, col(0)]
                k = hg_ref[rows, col(2)]
                v = hg_ref[rows, col(3)].astype(BF16)
                gate = hg_ref[rows, col(4)]
                b = b_all[:, lanes]
                ref = b[mid - 1:mid]
                last = b[chunk - 1:chunk]
                st = st_ref[h]
                inter = lax.dot_general((q * jnp.exp(b)).astype(BF16), st.astype(BF16),
                                        (((1,), (1,)), ((), ())), preferred_element_type=F32)
                if fast:
                    qh = (q * jnp.exp(b - ref)).astype(BF16)
                    kh = (k * jnp.exp(ref - b)).astype(BF16)
                    sc = lax.dot_general(qh, kh, (((1,), (1,)), ((), ())),
                                         preferred_element_type=F32)
                    worst = jnp.maximum(worst, jnp.maximum(b[0:1] - ref, ref - last))
                else:
                    sc = direct_scores(q, k, b)
                sc = jnp.where(causal, sc, 0.0).astype(BF16)
                o = inter + jnp.dot(sc, v, preferred_element_type=F32)
                kd = (k * jnp.exp(last - b)).astype(BF16)
                st_ref[h] = st * jnp.exp(last) + lax.dot_general(
                    v, kd, (((0,), (0,)), ((), ())), preferred_element_type=F32)
                o = _rms(o, gn_ref[:, lanes]) * (gate * jax.nn.sigmoid(gate))
                o_ref[rows, lanes] = o.astype(o_ref.dtype)
            return worst

        return lax.fori_loop(0, n_chunks, chunk_step, jnp.zeros((1, LANES), F32))

    sstart_ref[...] = st_ref[...]
    worst = run(fast=True)

    @pl.when(jnp.max(worst) > SAFE_EXP)
    def _():
        st_ref[...] = sstart_ref[...]
        run(fast=False)

    @pl.when(t == pl.num_programs(1) - 1)
    def _():
        for h in range(HG_HEADS):
            sout_ref[0, h] = st_ref[h].T


def _hgrn(hg, s0, gn, batch, seq):
    chunk = min(CHUNK, seq)
    block = min(ROW_TILE, seq)
    n_blocks = seq // block
    state_spec = pl.BlockSpec((1, HG_HEADS, HG_HEAD_DIM, HG_HEAD_DIM), lambda b, t: (b, 0, 0, 0))
    return pl.pallas_call(
        functools.partial(_hgrn_body, chunk, block // chunk),
        grid=(batch, n_blocks),
        in_specs=[pl.BlockSpec((block, HG_COLS), lambda b, t: (b * n_blocks + t, 0)),
                  state_spec, _resident((1, HG_WIDTH))],
        out_specs=[pl.BlockSpec((block, HG_WIDTH), lambda b, t: (b * n_blocks + t, 0)),
                   state_spec],
        out_shape=[jax.ShapeDtypeStruct((batch * seq, HG_WIDTH), BF16),
                   jax.ShapeDtypeStruct((batch, HG_HEADS, HG_HEAD_DIM, HG_HEAD_DIM), F32)],
        scratch_shapes=[pltpu.VMEM((HG_HEADS, HG_HEAD_DIM, HG_HEAD_DIM), F32),
                        pltpu.VMEM((HG_HEADS, HG_HEAD_DIM, HG_HEAD_DIM), F32)],
        compiler_params=pltpu.CompilerParams(dimension_semantics=("parallel", "arbitrary"),
                                             vmem_limit_bytes=VMEM_LIMIT),
        name="hgrn_scan",
    )(hg, s0, gn)


def _softmax_pv(s, v_masked):
    m = jnp.max(s, axis=-1, keepdims=True)
    p = jnp.exp(s - m)
    l = jnp.sum(p, axis=-1, keepdims=True)
    return jnp.dot(p.astype(BF16), v_masked, preferred_element_type=F32) / l


def _attn_prompt_body(q_ref, k_ref, v_ref, bias_ref, gn_ref, o_ref):
    qb = pl.program_id(1)
    start = pl.multiple_of(qb * Q_BLOCK, Q_BLOCK)
    q_chunk = lax.broadcasted_iota(jnp.int32, (Q_BLOCK, KEY_WINDOW), 0) // CHUNK
    k_col = lax.broadcasted_iota(jnp.int32, (Q_BLOCK, KEY_WINDOW), 1)
    k_chunk = k_col // CHUNK
    valid = ((k_chunk >= q_chunk) & (k_chunk <= q_chunk + LEFT_CHUNKS)
             & (k_col + start >= ATT_REACH))
    lane = lax.broadcasted_iota(jnp.int32, (1, LANES), 1)
    outs = []
    for pair in range(ATT_HEADS // 2):
        lanes = slice(pair * LANES, (pair + 1) * LANES)
        qp = q_ref[:, lanes]
        kp = k_ref[0, pl.ds(start, KEY_WINDOW), lanes]
        vp = v_ref[0, pl.ds(start, KEY_WINDOW), lanes]
        acc = jnp.zeros((Q_BLOCK, LANES), F32)
        for sub in range(2):
            mine = (lane // ATT_HEAD_DIM) == sub
            s = lax.dot_general(jnp.where(mine, qp, 0), kp, (((1,), (1,)), ((), ())),
                                preferred_element_type=F32)
            s = jnp.where(valid, s + bias_ref[2 * pair + sub], NEG)
            acc = acc + _softmax_pv(s, jnp.where(mine, vp, 0))
        outs.append(acc)
    o = jnp.concatenate(outs, axis=-1)
    o_ref[...] = _rms(o, gn_ref[...]).astype(o_ref.dtype)


def _attn_prompt(q, kpad, vpad, bias, gn, batch, seq):
    n_q = seq // Q_BLOCK
    kv_spec = pl.BlockSpec((1, seq + ATT_REACH, ATT_WIDTH), lambda b, i: (b, 0, 0))
    return pl.pallas_call(
        _attn_prompt_body,
        grid=(batch, n_q),
        in_specs=[pl.BlockSpec((Q_BLOCK, ATT_WIDTH), lambda b, i: (b * n_q + i, 0)),
                  kv_spec, kv_spec, _resident((ATT_HEADS, Q_BLOCK, KEY_WINDOW)),
                  _resident((1, ATT_WIDTH))],
        out_specs=pl.BlockSpec((Q_BLOCK, ATT_WIDTH), lambda b, i: (b * n_q + i, 0)),
        out_shape=jax.ShapeDtypeStruct((batch * seq, ATT_WIDTH), BF16),
        compiler_params=pltpu.CompilerParams(dimension_semantics=("parallel", "arbitrary"),
                                             vmem_limit_bytes=VMEM_LIMIT),
        name="attn_prompt",
    )(q, kpad, vpad, bias, gn)


def _attn_sample_body(q_ref, k_ref, v_ref, ck_ref, cv_ref, bias_ref, gn_ref, o_ref):
    lane = lax.broadcasted_iota(jnp.int32, (1, LANES), 1)
    outs = []
    for pair in range(ATT_HEADS // 2):
        lanes = slice(pair * LANES, (pair + 1) * LANES)
        qp = q_ref[:, lanes]
        kp = jnp.concatenate([ck_ref[0, 0, :, lanes].astype(BF16), k_ref[:, lanes]], axis=0)
        vp = jnp.concatenate([cv_ref[0, 0, :, lanes].astype(BF16), v_ref[:, lanes]], axis=0)
        acc = jnp.zeros((q_ref.shape[0], LANES), F32)
        for sub in range(2):
            mine = (lane // ATT_HEAD_DIM) == sub
            s = lax.dot_general(jnp.where(mine, qp, 0), kp, (((1,), (1,)), ((), ())),
                                preferred_element_type=F32)
            acc = acc + _softmax_pv(s + bias_ref[2 * pair + sub], jnp.where(mine, vp, 0))
        outs.append(acc)
    o = jnp.concatenate(outs, axis=-1)
    o_ref[...] = _rms(o, gn_ref[...]).astype(o_ref.dtype)


def _attn_sample(layer, q, k, v, cache_k, cache_v, bias, gn, batch, seq):
    past = cache_k.shape[2]
    row_spec = pl.BlockSpec((seq, ATT_WIDTH), lambda b: (b, 0))
    cache_spec = pl.BlockSpec((1, 1, past, ATT_WIDTH), lambda b: (layer, b, 0, 0))
    return pl.pallas_call(
        _attn_sample_body,
        grid=(batch,),
        in_specs=[row_spec, row_spec, row_spec, cache_spec, cache_spec,
                  _resident((ATT_HEADS, seq, past + seq)), _resident((1, ATT_WIDTH))],
        out_specs=row_spec,
        out_shape=jax.ShapeDtypeStruct((batch * seq, ATT_WIDTH), BF16),
        compiler_params=pltpu.CompilerParams(dimension_semantics=("parallel",),
                                             vmem_limit_bytes=VMEM_LIMIT),
        name="attn_sample",
    )(q, k, v, cache_k, cache_v, bias, gn)


def _mlp_body(final, x_ref, oh_ref, oa_ref, wo_ref, g2_ref, wu_ref, wd_ref, gf_ref, y_ref):
    mixed = jnp.concatenate([oh_ref[...], oa_ref[...]], axis=-1)
    x = x_ref[...] + jnp.dot(mixed, wo_ref[...], preferred_element_type=F32)
    h = _rms(x, g2_ref[...]).astype(BF16)
    mlp = None
    for c in range(D_FF // D_MODEL):
        cols = slice(c * D_MODEL, (c + 1) * D_MODEL)
        u = jnp.maximum(jnp.dot(h, wu_ref[:, cols], preferred_element_type=F32), 0.0)
        d = jnp.dot((u * u).astype(BF16), wd_ref[cols, :], preferred_element_type=F32)
        mlp = d if mlp is None else mlp + d
    x = x + mlp
    if final:
        x = _rms(x, gf_ref[...])
    y_ref[...] = x


def _mlp(final, x, oh, oa, wo, g2, wu, wd, gf):
    rows = x.shape[0]
    row_spec = lambda width: pl.BlockSpec((ROW_TILE, width), lambda i: (i, 0))
    return pl.pallas_call(
        functools.partial(_mlp_body, final),
        grid=(rows // ROW_TILE,),
        in_specs=[row_spec(D_MODEL), row_spec(HG_WIDTH), row_spec(ATT_WIDTH),
                  _resident((D_MODEL, D_MODEL)), _resident((1, D_MODEL)),
                  _resident((D_MODEL, D_FF)), _resident((D_FF, D_MODEL)), _resident((1, D_MODEL))],
        out_specs=row_spec(D_MODEL),
        out_shape=jax.ShapeDtypeStruct((rows, D_MODEL), F32),
        compiler_params=pltpu.CompilerParams(dimension_semantics=("parallel",),
                                             vmem_limit_bytes=VMEM_LIMIT),
        name="out_mlp",
    )(x, oh, oa, wo, g2, wu, wd, gf)


def _bias_table(rel_bias_l, n_q, n_k, k_offset):
    rel = jnp.clip((jnp.arange(n_k)[None, :] - k_offset) - jnp.arange(n_q)[:, None],
                   -REL_CLIP, REL_CLIP) + REL_CLIP
    return rel_bias_l.astype(F32)[:, rel]


def kernel(x_prompt, x_sample, state_hgrn, cache_k, cache_v, lb_param, norm1_g, w_in, hg_norm_g,
           rel_bias, att_norm_g, w_out, norm2_g, w_up, w_down, final_norm_g):
    batch, seq, _ = x_prompt.shape
    dbatch, dseq, _ = x_sample.shape
    past = cache_k.shape[2]
    keep = min(ATT_REACH, seq)
    xp = x_prompt.reshape(batch * seq, D_MODEL)
    xs = x_sample.reshape(dbatch * dseq, D_MODEL)
    ck = cache_k.reshape(DEPTH, dbatch, past, ATT_WIDTH)
    cv = cache_v.reshape(DEPTH, dbatch, past, ATT_WIDTH)
    zero_state = jnp.zeros((batch, HG_HEADS, HG_HEAD_DIM, HG_HEAD_DIM), F32)
    gf = final_norm_g.reshape(1, D_MODEL)
    row = lambda a, l: a[l].reshape(1, -1)
    pad = ((0, 0), (ATT_REACH, 0), (0, 0))

    sp_l, kp_l, vp_l, ss_l, ks_l, vs_l = [], [], [], [], [], []
    for l in range(DEPTH):
        w_in_l = w_in[l].astype(BF16)
        w_out_l = w_out[l].astype(BF16)
        w_up_l = w_up[l].astype(BF16)
        w_down_l = w_down[l].astype(BF16)
        g1, gh, ga, g2 = row(norm1_g, l), row(hg_norm_g, l), row(att_norm_g, l), row(norm2_g, l)
        final = l == DEPTH - 1

        hg, aq, ak, av, kf, vf = _inproj(l, xp, g1, w_in_l, lb_param)
        oh, sp = _hgrn(hg, zero_state, gh, batch, seq)
        kpad = jnp.pad(ak.reshape(batch, seq, ATT_WIDTH), pad)
        vpad = jnp.pad(av.reshape(batch, seq, ATT_WIDTH), pad)
        bias_p = _bias_table(rel_bias[l], Q_BLOCK, KEY_WINDOW, ATT_REACH)
        oa = _attn_prompt(aq, kpad, vpad, bias_p, ga, batch, seq)
        xp = _mlp(final, xp, oh, oa, w_out_l, g2, w_up_l, w_down_l, gf)
        sp_l.append(sp)
        kp_l.append(kf.reshape(batch, seq, ATT_HEADS, ATT_HEAD_DIM)[:, seq - keep:])
        vp_l.append(vf.reshape(batch, seq, ATT_HEADS, ATT_HEAD_DIM)[:, seq - keep:])

        hg, aq, ak, av, kf, vf = _inproj(l, xs, g1, w_in_l, lb_param)
        oh, ss = _hgrn(hg, state_hgrn[l], gh, dbatch, dseq)
        bias_s = _bias_table(rel_bias[l], dseq, past + dseq, past)
        oa = _attn_sample(l, aq, ak, av, ck, cv, bias_s, ga, dbatch, dseq)
        xs = _mlp(final, xs, oh, oa, w_out_l, g2, w_up_l, w_down_l, gf)
        ss_l.append(ss)
        ks_l.append(kf.reshape(dbatch, dseq, ATT_HEADS, ATT_HEAD_DIM))
        vs_l.append(vf.reshape(dbatch, dseq, ATT_HEADS, ATT_HEAD_DIM))

    return (xp.reshape(batch, seq, D_MODEL), xs.reshape(dbatch, dseq, D_MODEL),
            jnp.stack(sp_l), jnp.stack(kp_l), jnp.stack(vp_l),
            jnp.stack(ss_l), jnp.stack(ks_l), jnp.stack(vs_l))
```

```python
import functools
import math

import jax
import jax.numpy as jnp
from jax import lax
from jax.experimental import pallas as pl
from jax.experimental.pallas import tpu as pltpu

F32 = jnp.float32
BF16 = jnp.bfloat16

D_MODEL = 1024
DEPTH = 2
CHUNK = 64
HG_WIDTH = 512
HG_HEAD_DIM = 128
HG_HEADS = 4
ATT_WIDTH = 512
ATT_HEAD_DIM = 64
ATT_HEADS = 8
LEFT_CHUNKS = 8
ATT_REACH = LEFT_CHUNKS * CHUNK
REL_CLIP = 128
D_FF = 4 * D_MODEL
IN_WIDTH = 4 * HG_WIDTH + 3 * ATT_WIDTH
EPS = 1e-6
NEG = -1e30

LANES = 128
ROW_TILE = 512
Q_BLOCK = 256
KEY_WINDOW = ATT_REACH + Q_BLOCK
EXT_LEN = 1024
HG_COLS = 5 * HG_WIDTH
SAFE_EXP = 80.0
VMEM_LIMIT = 60 * 1024 * 1024


def _resident(shape):
    nd = len(shape)
    return pl.BlockSpec(shape, lambda *_: (0,) * nd, pipeline_mode=pl.Buffered(1))


def _rms(x, g):
    ms = jnp.mean(x * x, axis=-1, keepdims=True)
    return x * lax.rsqrt(ms + EPS) * g


def _inproj_body(layer, cache_period, x_ref, g_ref, w_ref, lbp_ref, hg_ref, aq_ref, ak_ref,
                 av_ref, kf_ref, vf_ref):
    h = _rms(x_ref[...], g_ref[...]).astype(BF16)

    def proj(sec):
        return jnp.dot(h, w_ref[:, sec * HG_WIDTH:(sec + 1) * HG_WIDTH],
                       preferred_element_type=F32)

    lp = lbp_ref[...]
    e = jnp.exp(lp - jnp.max(lp, axis=0, keepdims=True))
    sm = e / jnp.sum(e, axis=0, keepdims=True)
    lb = sm[0:1]
    for i in range(1, layer + 1):
        lb = lb + sm[i:i + 1]
    lb = lb - sm[0:1]

    f = lb + (1.0 - lb) * jax.nn.sigmoid(proj(1))
    hg_ref[:, 0 * HG_WIDTH:1 * HG_WIDTH] = proj(0)
    hg_ref[:, 1 * HG_WIDTH:2 * HG_WIDTH] = jnp.log(f)
    hg_ref[:, 2 * HG_WIDTH:3 * HG_WIDTH] = 1.0 - f
    hg_ref[:, 3 * HG_WIDTH:4 * HG_WIDTH] = proj(2)
    hg_ref[:, 4 * HG_WIDTH:5 * HG_WIDTH] = proj(3)
    aq_ref[...] = (proj(4) * (1.0 / math.sqrt(ATT_HEAD_DIM))).astype(BF16)
    k = proj(5)
    v = proj(6)
    ak_ref[...] = k.astype(BF16)
    av_ref[...] = v.astype(BF16)

    @pl.when(pl.program_id(0) % cache_period == cache_period - 1)
    def _():
        kf_ref[...] = k
        vf_ref[...] = v


def _inproj(layer, x, g, w, lbp, cache_period):
    rows = x.shape[0]
    n_tiles = rows // ROW_TILE
    row_spec = lambda width: pl.BlockSpec((ROW_TILE, width), lambda i: (i, 0))
    cache_spec = pl.BlockSpec((ROW_TILE, ATT_WIDTH), lambda i: (i // cache_period, 0))
    cache_shape = jax.ShapeDtypeStruct((rows // cache_period, ATT_WIDTH), F32)
    return pl.pallas_call(
        functools.partial(_inproj_body, layer, cache_period),
        grid=(n_tiles,),
        in_specs=[row_spec(D_MODEL), _resident((1, D_MODEL)), _resident((D_MODEL, IN_WIDTH)),
                  _resident((DEPTH, HG_WIDTH))],
        out_specs=[row_spec(HG_COLS), row_spec(ATT_WIDTH), row_spec(ATT_WIDTH),
                   row_spec(ATT_WIDTH), cache_spec, cache_spec],
        out_shape=[jax.ShapeDtypeStruct((rows, HG_COLS), F32),
                   jax.ShapeDtypeStruct((rows, ATT_WIDTH), BF16),
                   jax.ShapeDtypeStruct((rows, ATT_WIDTH), BF16),
                   jax.ShapeDtypeStruct((rows, ATT_WIDTH), BF16),
                   cache_shape, cache_shape],
        compiler_params=pltpu.CompilerParams(dimension_semantics=("arbitrary",),
                                             vmem_limit_bytes=VMEM_LIMIT),
        name="inproj",
    )(x, g, w, lbp)


def _split3(x):
    hi = x.astype(BF16)
    r1 = x - hi.astype(F32)
    mid = r1.astype(BF16)
    lo = (r1 - mid.astype(F32)).astype(BF16)
    return hi, mid, lo


def _hgrn_body(chunk, n_chunks, hg_ref, s0_ref, gn_ref, o_ref, sout_ref, st_ref, sstart_ref):
    t = pl.program_id(1)
    mid = chunk // 2

    @pl.when(t == 0)
    def _():
        for h in range(HG_HEADS):
            st_ref[h] = s0_ref[0, h].T

    row_i = lax.broadcasted_iota(jnp.int32, (chunk, chunk), 0)
    col_i = lax.broadcasted_iota(jnp.int32, (chunk, chunk), 1)
    causal = row_i >= col_i
    tri = jnp.where(causal, 1.0, 0.0).astype(BF16)

    def direct_scores(q, k, b):
        lane = lax.broadcasted_iota(jnp.int32, (chunk, LANES), 1)

        def body(s, sc):
            sel = lax.broadcasted_iota(jnp.int32, (chunk, 1), 0) == s
            bs = jnp.sum(jnp.where(sel, b, 0.0), axis=0, keepdims=True)
            ks = jnp.sum(jnp.where(sel, k, 0.0), axis=0, keepdims=True)
            col = jnp.sum(q * ks * jnp.exp(jnp.minimum(b - bs, 0.0)), axis=-1, keepdims=True)
            return jnp.where(lane == s, col, sc)

        sc = lax.fori_loop(0, chunk, body, jnp.zeros((chunk, LANES), F32))
        return sc[:, :chunk]

    def run(fast):
        def chunk_step(c, worst):
            rows = pl.ds(pl.multiple_of(c * chunk, chunk), chunk)
            g_all = hg_ref[rows, 1 * HG_WIDTH:2 * HG_WIDTH]
            b_all = sum(jnp.dot(tri, part, preferred_element_type=F32) for part in _split3(g_all))
            for h in range(HG_HEADS):
                lanes = slice(h * HG_HEAD_DIM, (h + 1) * HG_HEAD_DIM)
                col = lambda sec: slice(sec * HG_WIDTH + h * HG_HEAD_DIM,
                                        sec * HG_WIDTH + (h + 1) * HG_HEAD_DIM)
                q = hg_ref[rows, col(0)]
                k = hg_ref[rows, col(2)]
                v = hg_ref[rows, col(3)].astype(BF16)
                gate = hg_ref[rows, col(4)]
                b = b_all[:, lanes]
                ref = b[mid - 1:mid]
                last = b[chunk - 1:chunk]
                st = st_ref[h]
                inter = lax.dot_general((q * jnp.exp(b)).astype(BF16), st.astype(BF16),
                                        (((1,), (1,)), ((), ())), preferred_element_type=F32)
                if fast:
                    qh = (q * jnp.exp(b - ref)).astype(BF16)
                    kh = (k * jnp.exp(ref - b)).astype(BF16)
                    sc = lax.dot_general(qh, kh, (((1,), (1,)), ((), ())),
                                         preferred_element_type=F32)
                    worst = jnp.maximum(worst, jnp.maximum(b[0:1] - ref, ref - last))
                else:
                    sc = direct_scores(q, k, b)
                sc = jnp.where(causal, sc, 0.0).astype(BF16)
                o = inter + jnp.dot(sc, v, preferred_element_type=F32)
                kd = (k * jnp.exp(last - b)).astype(BF16)
                st_ref[h] = st * jnp.exp(last) + lax.dot_general(
                    v, kd, (((0,), (0,)), ((), ())), preferred_element_type=F32)
                o = _rms(o, gn_ref[:, lanes]) * (gate * jax.nn.sigmoid(gate))
                o_ref[rows, lanes] = o.astype(o_ref.dtype)
            return worst

        return lax.fori_loop(0, n_chunks, chunk_step, jnp.zeros((1, LANES), F32))

    sstart_ref[...] = st_ref[...]
    worst = run(fast=True)

    @pl.when(jnp.max(worst) > SAFE_EXP)
    def _():
        st_ref[...] = sstart_ref[...]
        run(fast=False)

    @pl.when(t == pl.num_programs(1) - 1)
    def _():
        for h in range(HG_HEADS):
            sout_ref[0, h] = st_ref[h].T


def _hgrn(hg, s0, gn, batch, seq):
    chunk = min(CHUNK, seq)
    block = min(ROW_TILE, seq)
    n_blocks = seq // block
    state_spec = pl.BlockSpec((1, HG_HEADS, HG_HEAD_DIM, HG_HEAD_DIM), lambda b, t: (b, 0, 0, 0))
    return pl.pallas_call(
        functools.partial(_hgrn_body, chunk, block // chunk),
        grid=(batch, n_blocks),
        in_specs=[pl.BlockSpec((block, HG_COLS), lambda b, t: (b * n_blocks + t, 0)),
                  state_spec, _resident((1, HG_WIDTH))],
        out_specs=[pl.BlockSpec((block, HG_WIDTH), lambda b, t: (b * n_blocks + t, 0)),
                   state_spec],
        out_shape=[jax.ShapeDtypeStruct((batch * seq, HG_WIDTH), BF16),
                   jax.ShapeDtypeStruct((batch, HG_HEADS, HG_HEAD_DIM, HG_HEAD_DIM), F32)],
        scratch_shapes=[pltpu.VMEM((HG_HEADS, HG_HEAD_DIM, HG_HEAD_DIM), F32),
                        pltpu.VMEM((HG_HEADS, HG_HEAD_DIM, HG_HEAD_DIM), F32)],
        compiler_params=pltpu.CompilerParams(dimension_semantics=("parallel", "arbitrary"),
                                             vmem_limit_bytes=VMEM_LIMIT),
        name="hgrn_scan",
    )(hg, s0, gn)


def _softmax_pv(s, v_masked):
    m = jnp.max(s, axis=-1, keepdims=True)
    p = jnp.exp(s - m)
    l = jnp.sum(p, axis=-1, keepdims=True)
    return jnp.dot(p.astype(BF16), v_masked, preferred_element_type=F32) / l


def _toeplitz(ext_row, n_q):
    ext = jnp.broadcast_to(ext_row, (n_q, EXT_LEN))
    return pltpu.roll(ext, EXT_LEN - n_q + 1, 1, stride=1, stride_axis=0)


def _attn_prompt_body(q_ref, k_ref, v_ref, ext_ref, gn_ref, o_ref, bias_ref):
    qb = pl.program_id(1)
    start = pl.multiple_of(qb * Q_BLOCK, Q_BLOCK)

    @pl.when(qb == 0)
    def _():
        q_chunk = lax.broadcasted_iota(jnp.int32, (Q_BLOCK, KEY_WINDOW), 0) // CHUNK
        k_chunk = lax.broadcasted_iota(jnp.int32, (Q_BLOCK, KEY_WINDOW), 1) // CHUNK
        band = (k_chunk >= q_chunk) & (k_chunk <= q_chunk + LEFT_CHUNKS)
        for h in range(ATT_HEADS):
            t = _toeplitz(ext_ref[h:h + 1, :], Q_BLOCK)
            bias_ref[h] = jnp.where(band, t[:, :KEY_WINDOW], NEG)

    k_col = lax.broadcasted_iota(jnp.int32, (1, KEY_WINDOW), 1)
    valid = k_col + start >= ATT_REACH
    lane = lax.broadcasted_iota(jnp.int32, (1, LANES), 1)
    outs = []
    for pair in range(ATT_HEADS // 2):
        lanes = slice(pair * LANES, (pair + 1) * LANES)
        qp = q_ref[:, lanes]
        kp = k_ref[0, pl.ds(start, KEY_WINDOW), lanes]
        vp = v_ref[0, pl.ds(start, KEY_WINDOW), lanes]
        acc = jnp.zeros((Q_BLOCK, LANES), F32)
        for sub in range(2):
            mine = (lane // ATT_HEAD_DIM) == sub
            s = lax.dot_general(jnp.where(mine, qp, 0), kp, (((1,), (1,)), ((), ())),
                                preferred_element_type=F32)
            s = jnp.where(valid, s + bias_ref[2 * pair + sub], NEG)
            acc = acc + _softmax_pv(s, jnp.where(mine, vp, 0))
        outs.append(acc)
    o = jnp.concatenate(outs, axis=-1)
    o_ref[...] = _rms(o, gn_ref[...]).astype(o_ref.dtype)


def _attn_prompt(q, kpad, vpad, ext, gn, batch, seq):
    n_q = seq // Q_BLOCK
    kv_spec = pl.BlockSpec((1, seq + ATT_REACH, ATT_WIDTH), lambda b, i: (b, 0, 0))
    return pl.pallas_call(
        _attn_prompt_body,
        grid=(batch, n_q),
        in_specs=[pl.BlockSpec((Q_BLOCK, ATT_WIDTH), lambda b, i: (b * n_q + i, 0)),
                  kv_spec, kv_spec, _resident((ATT_HEADS, EXT_LEN)), _resident((1, ATT_WIDTH))],
        out_specs=pl.BlockSpec((Q_BLOCK, ATT_WIDTH), lambda b, i: (b * n_q + i, 0)),
        out_shape=jax.ShapeDtypeStruct((batch * seq, ATT_WIDTH), BF16),
        scratch_shapes=[pltpu.VMEM((ATT_HEADS, Q_BLOCK, KEY_WINDOW), F32)],
        compiler_params=pltpu.CompilerParams(dimension_semantics=("parallel", "arbitrary"),
                                             vmem_limit_bytes=VMEM_LIMIT),
        name="attn_prompt",
    )(q, kpad, vpad, ext, gn)


def _attn_sample_body(q_ref, k_ref, v_ref, ck_ref, cv_ref, ext_ref, gn_ref, o_ref):
    n_q = q_ref.shape[0]
    n_k = ck_ref.shape[2] + n_q
    lane = lax.broadcasted_iota(jnp.int32, (1, LANES), 1)
    outs = []
    for pair in range(ATT_HEADS // 2):
        lanes = slice(pair * LANES, (pair + 1) * LANES)
        qp = q_ref[:, lanes]
        kp = jnp.concatenate([ck_ref[0, 0, :, lanes].astype(BF16), k_ref[:, lanes]], axis=0)
        vp = jnp.concatenate([cv_ref[0, 0, :, lanes].astype(BF16), v_ref[:, lanes]], axis=0)
        acc = jnp.zeros((n_q, LANES), F32)
        for sub in range(2):
            h = 2 * pair + sub
            mine = (lane // ATT_HEAD_DIM) == sub
            s = lax.dot_general(jnp.where(mine, qp, 0), kp, (((1,), (1,)), ((), ())),
                                preferred_element_type=F32)
            bias = _toeplitz(ext_ref[h:h + 1, :], n_q)[:, :n_k]
            acc = acc + _softmax_pv(s + bias, jnp.where(mine, vp, 0))
        outs.append(acc)
    o = jnp.concatenate(outs, axis=-1)
    o_ref[...] = _rms(o, gn_ref[...]).astype(o_ref.dtype)


def _attn_sample(layer, q, k, v, cache_k, cache_v, ext, gn, batch, seq):
    past = cache_k.shape[2]
    row_spec = pl.BlockSpec((seq, ATT_WIDTH), lambda b: (b, 0))
    cache_spec = pl.BlockSpec((1, 1, past, ATT_WIDTH), lambda b: (layer, b, 0, 0))
    return pl.pallas_call(
        _attn_sample_body,
        grid=(batch,),
        in_specs=[row_spec, row_spec, row_spec, cache_spec, cache_spec,
                  _resident((ATT_HEADS, EXT_LEN)), _resident((1, ATT_WIDTH))],
        out_specs=row_spec,
        out_shape=jax.ShapeDtypeStruct((batch * seq, ATT_WIDTH), BF16),
        compiler_params=pltpu.CompilerParams(dimension_semantics=("parallel",),
                                             vmem_limit_bytes=VMEM_LIMIT),
        name="attn_sample",
    )(q, k, v, cache_k, cache_v, ext, gn)


def _mlp_body(final, x_ref, oh_ref, oa_ref, wo_ref, g2_ref, wu_ref, wd_ref, gf_ref, y_ref):
    mixed = jnp.concatenate([oh_ref[...], oa_ref[...]], axis=-1)
    x = x_ref[...] + jnp.dot(mixed, wo_ref[...], preferred_element_type=F32)
    h = _rms(x, g2_ref[...]).astype(BF16)
    mlp = None
    for c in range(D_FF // D_MODEL):
        cols = slice(c * D_MODEL, (c + 1) * D_MODEL)
        u = jnp.maximum(jnp.dot(h, wu_ref[:, cols], preferred_element_type=F32), 0.0)
        d = jnp.dot((u * u).astype(BF16), wd_ref[cols, :], preferred_element_type=F32)
        mlp = d if mlp is None else mlp + d
    x = x + mlp
    if final:
        x = _rms(x, gf_ref[...])
    y_ref[...] = x


def _mlp(final, x, oh, oa, wo, g2, wu, wd, gf):
    rows = x.shape[0]
    row_spec = lambda width: pl.BlockSpec((ROW_TILE, width), lambda i: (i, 0))
    return pl.pallas_call(
        functools.partial(_mlp_body, final),
        grid=(rows // ROW_TILE,),
        in_specs=[row_spec(D_MODEL), row_spec(HG_WIDTH), row_spec(ATT_WIDTH),
                  _resident((D_MODEL, D_MODEL)), _resident((1, D_MODEL)),
                  _resident((D_MODEL, D_FF)), _resident((D_FF, D_MODEL)), _resident((1, D_MODEL))],
        out_specs=row_spec(D_MODEL),
        out_shape=jax.ShapeDtypeStruct((rows, D_MODEL), F32),
        compiler_params=pltpu.CompilerParams(dimension_semantics=("parallel",),
                                             vmem_limit_bytes=VMEM_LIMIT),
        name="out_mlp",
    )(x, oh, oa, wo, g2, wu, wd, gf)


def _bias_by_distance(rel_bias_l, n_q, k_offset):
    left = n_q - 1 + k_offset - REL_CLIP
    right = EXT_LEN - (2 * REL_CLIP + 1) - left
    return jnp.pad(rel_bias_l.astype(F32), ((0, 0), (left, right)), mode="edge")


def kernel(x_prompt, x_sample, state_hgrn, cache_k, cache_v, lb_param, norm1_g, w_in, hg_norm_g,
           rel_bias, att_norm_g, w_out, norm2_g, w_up, w_down, final_norm_g):
    batch, seq, _ = x_prompt.shape
    dbatch, dseq, _ = x_sample.shape
    past = cache_k.shape[2]
    assert seq % ROW_TILE == 0 and min(ATT_REACH, seq) == ROW_TILE
    assert dbatch * dseq == ROW_TILE and dseq <= ATT_REACH
    xp = x_prompt.reshape(batch * seq, D_MODEL)
    xs = x_sample.reshape(dbatch * dseq, D_MODEL)
    ck = cache_k.reshape(DEPTH, dbatch, past, ATT_WIDTH)
    cv = cache_v.reshape(DEPTH, dbatch, past, ATT_WIDTH)
    zero_state = jnp.zeros((batch, HG_HEADS, HG_HEAD_DIM, HG_HEAD_DIM), F32)
    gf = final_norm_g.reshape(1, D_MODEL)
    row = lambda a, l: a[l].reshape(1, -1)
    pad = ((0, 0), (ATT_REACH, 0), (0, 0))

    sp_l, kp_l, vp_l, ss_l, ks_l, vs_l = [], [], [], [], [], []
    for l in range(DEPTH):
        w_in_l = w_in[l].astype(BF16)
        w_out_l = w_out[l].astype(BF16)
        w_up_l = w_up[l].astype(BF16)
        w_down_l = w_down[l].astype(BF16)
        g1, gh, ga, g2 = row(norm1_g, l), row(hg_norm_g, l), row(att_norm_g, l), row(norm2_g, l)
        final = l == DEPTH - 1

        hg, aq, ak, av, kf, vf = _inproj(l, xp, g1, w_in_l, lb_param, seq // ROW_TILE)
        oh, sp = _hgrn(hg, zero_state, gh, batch, seq)
        kpad = jnp.pad(ak.reshape(batch, seq, ATT_WIDTH), pad)
        vpad = jnp.pad(av.reshape(batch, seq, ATT_WIDTH), pad)
        ext_p = _bias_by_distance(rel_bias[l], Q_BLOCK, ATT_REACH)
        oa = _attn_prompt(aq, kpad, vpad, ext_p, ga, batch, seq)
        xp = _mlp(final, xp, oh, oa, w_out_l, g2, w_up_l, w_down_l, gf)
        sp_l.append(sp)
        kp_l.append(kf.reshape(batch, ROW_TILE, ATT_HEADS, ATT_HEAD_DIM))
        vp_l.append(vf.reshape(batch, ROW_TILE, ATT_HEADS, ATT_HEAD_DIM))

        hg, aq, ak, av, kf, vf = _inproj(l, xs, g1, w_in_l, lb_param, 1)
        oh, ss = _hgrn(hg, state_hgrn[l], gh, dbatch, dseq)
        ext_s = _bias_by_distance(rel_bias[l], dseq, past)
        oa = _attn_sample(l, aq, ak, av, ck, cv, ext_s, ga, dbatch, dseq)
        xs = _mlp(final, xs, oh, oa, w_out_l, g2, w_up_l, w_down_l, gf)
        ss_l.append(ss)
        ks_l.append(kf.reshape(dbatch, dseq, ATT_HEADS, ATT_HEAD_DIM))
        vs_l.append(vf.reshape(dbatch, dseq, ATT_HEADS, ATT_HEAD_DIM))

    return (xp.reshape(batch, seq, D_MODEL), xs.reshape(dbatch, dseq, D_MODEL),
            jnp.stack(sp_l), jnp.stack(kp_l), jnp.stack(vp_l),
            jnp.stack(ss_l), jnp.stack(ks_l), jnp.stack(vs_l))
```

```python
import functools
import math

import jax
import jax.numpy as jnp
from jax import lax
from jax.experimental import pallas as pl
from jax.experimental.pallas import tpu as pltpu

F32 = jnp.float32
BF16 = jnp.bfloat16

D_MODEL = 1024
DEPTH = 2
CHUNK = 64
HG_WIDTH = 512
HG_HEAD_DIM = 128
HG_HEADS = 4
ATT_WIDTH = 512
ATT_HEAD_DIM = 64
ATT_HEADS = 8
LEFT_CHUNKS = 8
ATT_REACH = LEFT_CHUNKS * CHUNK
REL_CLIP = 128
D_FF = 4 * D_MODEL
IN_WIDTH = 4 * HG_WIDTH + 3 * ATT_WIDTH
EPS = 1e-6
NEG = -1e30
LOG2E = math.log2(math.e)

LANES = 128
ROW_TILE = 512
Q_BLOCK = 256
KEY_WINDOW = ATT_REACH + Q_BLOCK
EXT_LEN = 1024
HG_COLS = 5 * HG_WIDTH
SAFE_EXP = 80.0
VMEM_LIMIT = 60 * 1024 * 1024


def _resident(shape):
    nd = len(shape)
    return pl.BlockSpec(shape, lambda *_: (0,) * nd, pipeline_mode=pl.Buffered(1))


def _rms(x, g):
    ms = jnp.mean(x * x, axis=-1, keepdims=True)
    return x * lax.rsqrt(ms + EPS) * g


def _inproj_body(layer, cache_period, x_ref, g_ref, w_ref, lbp_ref, hg_ref, aq_ref, ak_ref,
                 av_ref, kf_ref, vf_ref):
    h = _rms(x_ref[...], g_ref[...]).astype(BF16)

    def proj(sec):
        return jnp.dot(h, w_ref[:, sec * HG_WIDTH:(sec + 1) * HG_WIDTH],
                       preferred_element_type=F32)

    lp = lbp_ref[...]
    e = jnp.exp(lp - jnp.max(lp, axis=0, keepdims=True))
    sm = e / jnp.sum(e, axis=0, keepdims=True)
    lb = sm[0:1]
    for i in range(1, layer + 1):
        lb = lb + sm[i:i + 1]
    lb = lb - sm[0:1]

    f = lb + (1.0 - lb) * jax.nn.sigmoid(proj(1))
    hg_ref[:, 0 * HG_WIDTH:1 * HG_WIDTH] = proj(0)
    hg_ref[:, 1 * HG_WIDTH:2 * HG_WIDTH] = jnp.log(f)
    hg_ref[:, 2 * HG_WIDTH:3 * HG_WIDTH] = 1.0 - f
    hg_ref[:, 3 * HG_WIDTH:4 * HG_WIDTH] = proj(2)
    hg_ref[:, 4 * HG_WIDTH:5 * HG_WIDTH] = proj(3)
    aq_ref[...] = (proj(4) * (LOG2E / math.sqrt(ATT_HEAD_DIM))).astype(BF16)
    k = proj(5)
    v = proj(6)
    ak_ref[...] = k.astype(BF16)
    av_ref[...] = v.astype(BF16)

    @pl.when(pl.program_id(0) % cache_period == cache_period - 1)
    def _():
        kf_ref[...] = k
        vf_ref[...] = v


def _inproj(layer, x, g, w, lbp, cache_period):
    rows = x.shape[0]
    n_tiles = rows // ROW_TILE
    row_spec = lambda width: pl.BlockSpec((ROW_TILE, width), lambda i: (i, 0))
    cache_spec = pl.BlockSpec((ROW_TILE, ATT_WIDTH), lambda i: (i // cache_period, 0))
    cache_shape = jax.ShapeDtypeStruct((rows // cache_period, ATT_WIDTH), F32)
    return pl.pallas_call(
        functools.partial(_inproj_body, layer, cache_period),
        grid=(n_tiles,),
        in_specs=[row_spec(D_MODEL), _resident((1, D_MODEL)), _resident((D_MODEL, IN_WIDTH)),
                  _resident((DEPTH, HG_WIDTH))],
        out_specs=[row_spec(HG_COLS), row_spec(ATT_WIDTH), row_spec(ATT_WIDTH),
                   row_spec(ATT_WIDTH), cache_spec, cache_spec],
        out_shape=[jax.ShapeDtypeStruct((rows, HG_COLS), F32),
                   jax.ShapeDtypeStruct((rows, ATT_WIDTH), BF16),
                   jax.ShapeDtypeStruct((rows, ATT_WIDTH), BF16),
                   jax.ShapeDtypeStruct((rows, ATT_WIDTH), BF16),
                   cache_shape, cache_shape],
        compiler_params=pltpu.CompilerParams(dimension_semantics=("arbitrary",),
                                             vmem_limit_bytes=VMEM_LIMIT),
        name="inproj",
    )(x, g, w, lbp)


def _split3(x):
    hi = x.astype(BF16)
    r1 = x - hi.astype(F32)
    mid = r1.astype(BF16)
    lo = (r1 - mid.astype(F32)).astype(BF16)
    return hi, mid, lo


def _hgrn_body(chunk, n_chunks, hg_ref, s0_ref, gn_ref, o_ref, sout_ref, st_ref, sstart_ref):
    t = pl.program_id(1)
    mid = chunk // 2

    @pl.when(t == 0)
    def _():
        for h in range(HG_HEADS):
            st_ref[h] = s0_ref[0, h].T

    row_i = lax.broadcasted_iota(jnp.int32, (chunk, chunk), 0)
    col_i = lax.broadcasted_iota(jnp.int32, (chunk, chunk), 1)
    causal = row_i >= col_i
    tri = jnp.where(causal, 1.0, 0.0).astype(BF16)

    def direct_scores(q, k, b):
        lane = lax.broadcasted_iota(jnp.int32, (chunk, LANES), 1)

        def body(s, sc):
            sel = lax.broadcasted_iota(jnp.int32, (chunk, 1), 0) == s
            bs = jnp.sum(jnp.where(sel, b, 0.0), axis=0, keepdims=True)
            ks = jnp.sum(jnp.where(sel, k, 0.0), axis=0, keepdims=True)
            col = jnp.sum(q * ks * jnp.exp(jnp.minimum(b - bs, 0.0)), axis=-1, keepdims=True)
            return jnp.where(lane == s, col, sc)

        sc = lax.fori_loop(0, chunk, body, jnp.zeros((chunk, LANES), F32))
        return sc[:, :chunk]

    def run(fast):
        def chunk_step(c, worst):
            rows = pl.ds(pl.multiple_of(c * chunk, chunk), chunk)
            g_all = hg_ref[rows, 1 * HG_WIDTH:2 * HG_WIDTH]
            b_all = sum(jnp.dot(tri, part, preferred_element_type=F32) for part in _split3(g_all))
            for h in range(HG_HEADS):
                lanes = slice(h * HG_HEAD_DIM, (h + 1) * HG_HEAD_DIM)
                col = lambda sec: slice(sec * HG_WIDTH + h * HG_HEAD_DIM,
                                        sec * HG_WIDTH + (h + 1) * HG_HEAD_DIM)
                q = hg_ref[rows, col(0)]
                k = hg_ref[rows, col(2)]
                v = hg_ref[rows, col(3)].astype(BF16)
                gate = hg_ref[rows, col(4)]
                b = b_all[:, lanes]
                ref = b[mid - 1:mid]
                last = b[chunk - 1:chunk]
                st = st_ref[h]
                inter = lax.dot_general((q * jnp.exp(b)).astype(BF16), st.astype(BF16),
                                        (((1,), (1,)), ((), ())), preferred_element_type=F32)
                if fast:
                    qh = (q * jnp.exp(b - ref)).astype(BF16)
                    kh = (k * jnp.exp(ref - b)).astype(BF16)
                    sc = lax.dot_general(qh, kh, (((1,), (1,)), ((), ())),
                                         preferred_element_type=F32)
                    worst = jnp.maximum(worst, jnp.maximum(b[0:1] - ref, ref - last))
                else:
                    sc = direct_scores(q, k, b)
                sc = jnp.where(causal, sc, 0.0).astype(BF16)
                o = inter + jnp.dot(sc, v, preferred_element_type=F32)
                kd = (k * jnp.exp(last - b)).astype(BF16)
                st_ref[h] = st * jnp.exp(last) + lax.dot_general(
                    v, kd, (((0,), (0,)), ((), ())), preferred_element_type=F32)
                o = _rms(o, gn_ref[:, lanes]) * (gate * jax.nn.sigmoid(gate))
                o_ref[rows, lanes] = o.astype(o_ref.dtype)
            return worst

        return lax.fori_loop(0, n_chunks, chunk_step, jnp.zeros((1, LANES), F32))

    sstart_ref[...] = st_ref[...]
    worst = run(fast=True)

    @pl.when(jnp.max(worst) > SAFE_EXP)
    def _():
        st_ref[...] = sstart_ref[...]
        run(fast=False)

    @pl.when(t == pl.num_programs(1) - 1)
    def _():
        for h in range(HG_HEADS):
            sout_ref[0, h] = st_ref[h].T


def _hgrn(hg, s0, gn, batch, seq):
    chunk = min(CHUNK, seq)
    block = min(ROW_TILE, seq)
    n_blocks = seq // block
    state_spec = pl.BlockSpec((1, HG_HEADS, HG_HEAD_DIM, HG_HEAD_DIM), lambda b, t: (b, 0, 0, 0))
    return pl.pallas_call(
        functools.partial(_hgrn_body, chunk, block // chunk),
        grid=(batch, n_blocks),
        in_specs=[pl.BlockSpec((block, HG_COLS), lambda b, t: (b * n_blocks + t, 0)),
                  state_spec, _resident((1, HG_WIDTH))],
        out_specs=[pl.BlockSpec((block, HG_WIDTH), lambda b, t: (b * n_blocks + t, 0)),
                   state_spec],
        out_shape=[jax.ShapeDtypeStruct((batch * seq, HG_WIDTH), BF16),
                   jax.ShapeDtypeStruct((batch, HG_HEADS, HG_HEAD_DIM, HG_HEAD_DIM), F32)],
        scratch_shapes=[pltpu.VMEM((HG_HEADS, HG_HEAD_DIM, HG_HEAD_DIM), F32),
                        pltpu.VMEM((HG_HEADS, HG_HEAD_DIM, HG_HEAD_DIM), F32)],
        compiler_params=pltpu.CompilerParams(dimension_semantics=("parallel", "arbitrary"),
                                             vmem_limit_bytes=VMEM_LIMIT),
        name="hgrn_scan",
    )(hg, s0, gn)


def _softmax_pv(s, v_masked):
    m = jnp.max(s, axis=-1, keepdims=True)
    p = jnp.exp2(s - m)
    l = jnp.sum(p, axis=-1, keepdims=True)
    return jnp.dot(p.astype(BF16), v_masked, preferred_element_type=F32) / l


def _toeplitz(ext_row, n_q):
    ext = jnp.broadcast_to(ext_row, (n_q, EXT_LEN))
    return pltpu.roll(ext, EXT_LEN - n_q + 1, 1, stride=1, stride_axis=0)


def _reduce_rows(x, combine, finish, slab=CHUNK):
    acc = x[0:slab]
    for i in range(1, x.shape[0] // slab):
        acc = combine(acc, x[i * slab:(i + 1) * slab])
    return finish(acc, axis=0, keepdims=True)


def _attn_prompt_body(q_ref, k_ref, v_ref, ext_ref, gn_ref, o_ref, bias_ref):
    qb = pl.program_id(1)
    start = pl.multiple_of(qb * Q_BLOCK, Q_BLOCK)

    @pl.when(qb == 0)
    def _():
        q_chunk = lax.broadcasted_iota(jnp.int32, (Q_BLOCK, KEY_WINDOW), 0) // CHUNK
        k_chunk = lax.broadcasted_iota(jnp.int32, (Q_BLOCK, KEY_WINDOW), 1) // CHUNK
        band = (k_chunk >= q_chunk) & (k_chunk <= q_chunk + LEFT_CHUNKS)
        for h in range(ATT_HEADS):
            t = _toeplitz(ext_ref[h:h + 1, :], Q_BLOCK)[:, :KEY_WINDOW] * LOG2E
            bias_ref[h] = jnp.where(band, t, NEG).T

    def block(mask_prefix):
        lane = lax.broadcasted_iota(jnp.int32, (1, LANES), 1)
        if mask_prefix:
            k_row = lax.broadcasted_iota(jnp.int32, (KEY_WINDOW, Q_BLOCK), 0)
            valid = k_row + start >= ATT_REACH
        def pair_scores(pair):
            lanes = slice(pair * LANES, (pair + 1) * LANES)
            qp = q_ref[:, lanes]
            kp = k_ref[0, pl.ds(start, KEY_WINDOW), lanes]
            qq = jnp.concatenate([jnp.where((lane // ATT_HEAD_DIM) == sub, qp, 0)
                                  for sub in range(2)], axis=0)
            return lax.dot_general(kp, qq, (((1,), (1,)), ((), ())),
                                   preferred_element_type=F32)

        outs = []
        n_pairs = ATT_HEADS // 2
        s_next = pair_scores(0)
        for pair in range(n_pairs):
            lanes = slice(pair * LANES, (pair + 1) * LANES)
            s_pair = s_next
            if pair + 1 < n_pairs:
                s_next = pair_scores(pair + 1)
            vt = v_ref[0, pl.ds(start, KEY_WINDOW), lanes].T
            for sub in range(2):
                s = s_pair[:, sub * Q_BLOCK:(sub + 1) * Q_BLOCK] + bias_ref[2 * pair + sub]
                if mask_prefix:
                    s = jnp.where(valid, s, NEG)
                p = jnp.exp2(s - _reduce_rows(s, jnp.maximum, jnp.max))
                l = _reduce_rows(p, jnp.add, jnp.sum)
                o = jnp.dot(vt[sub * ATT_HEAD_DIM:(sub + 1) * ATT_HEAD_DIM], p.astype(BF16),
                            preferred_element_type=F32)
                outs.append(o / l)
        o = jnp.concatenate(outs, axis=0).T
        o_ref[...] = _rms(o, gn_ref[...]).astype(o_ref.dtype)

    n_prefix = ATT_REACH // Q_BLOCK

    @pl.when(qb < n_prefix)
    def _():
        block(True)

    @pl.when(qb >= n_prefix)
    def _():
        block(False)


def _attn_prompt(q, kpad, vpad, ext, gn, batch, seq):
    n_q = seq // Q_BLOCK
    kv_spec = pl.BlockSpec((1, seq + ATT_REACH, ATT_WIDTH), lambda b, i: (b, 0, 0))
    return pl.pallas_call(
        _attn_prompt_body,
        grid=(batch, n_q),
        in_specs=[pl.BlockSpec((Q_BLOCK, ATT_WIDTH), lambda b, i: (b * n_q + i, 0)),
                  kv_spec, kv_spec, _resident((ATT_HEADS, EXT_LEN)), _resident((1, ATT_WIDTH))],
        out_specs=pl.BlockSpec((Q_BLOCK, ATT_WIDTH), lambda b, i: (b * n_q + i, 0)),
        out_shape=jax.ShapeDtypeStruct((batch * seq, ATT_WIDTH), BF16),
        scratch_shapes=[pltpu.VMEM((ATT_HEADS, KEY_WINDOW, Q_BLOCK), F32)],
        compiler_params=pltpu.CompilerParams(dimension_semantics=("parallel", "arbitrary"),
                                             vmem_limit_bytes=VMEM_LIMIT),
        name="attn_prompt",
    )(q, kpad, vpad, ext, gn)


def _attn_sample_body(q_ref, k_ref, v_ref, ck_ref, cv_ref, ext_ref, gn_ref, o_ref):
    n_q = q_ref.shape[0]
    n_k = ck_ref.shape[2] + n_q
    lane = lax.broadcasted_iota(jnp.int32, (1, LANES), 1)
    outs = []
    for pair in range(ATT_HEADS // 2):
        lanes = slice(pair * LANES, (pair + 1) * LANES)
        qp = q_ref[:, lanes]
        kp = jnp.concatenate([ck_ref[0, 0, :, lanes].astype(BF16), k_ref[:, lanes]], axis=0)
        vp = jnp.concatenate([cv_ref[0, 0, :, lanes].astype(BF16), v_ref[:, lanes]], axis=0)
        acc = jnp.zeros((n_q, LANES), F32)
        for sub in range(2):
            h = 2 * pair + sub
            mine = (lane // ATT_HEAD_DIM) == sub
            s = lax.dot_general(jnp.where(mine, qp, 0), kp, (((1,), (1,)), ((), ())),
                                preferred_element_type=F32)
            bias = _toeplitz(ext_ref[h:h + 1, :], n_q)[:, :n_k] * LOG2E
            acc = acc + _softmax_pv(s + bias, jnp.where(mine, vp, 0))
        outs.append(acc)
    o = jnp.concatenate(outs, axis=-1)
    o_ref[...] = _rms(o, gn_ref[...]).astype(o_ref.dtype)


def _attn_sample(layer, q, k, v, cache_k, cache_v, ext, gn, batch, seq):
    past = cache_k.shape[2]
    row_spec = pl.BlockSpec((seq, ATT_WIDTH), lambda b: (b, 0))
    cache_spec = pl.BlockSpec((1, 1, past, ATT_WIDTH), lambda b: (layer, b, 0, 0))
    return pl.pallas_call(
        _attn_sample_body,
        grid=(batch,),
        in_specs=[row_spec, row_spec, row_spec, cache_spec, cache_spec,
                  _resident((ATT_HEADS, EXT_LEN)), _resident((1, ATT_WIDTH))],
        out_specs=row_spec,
        out_shape=jax.ShapeDtypeStruct((batch * seq, ATT_WIDTH), BF16),
        compiler_params=pltpu.CompilerParams(dimension_semantics=("parallel",),
                                             vmem_limit_bytes=VMEM_LIMIT),
        name="attn_sample",
    )(q, k, v, cache_k, cache_v, ext, gn)


def _mlp_body(final, x_ref, oh_ref, oa_ref, wo_ref, g2_ref, wu_ref, wd_ref, gf_ref, y_ref):
    mixed = jnp.concatenate([oh_ref[...], oa_ref[...]], axis=-1)
    x = x_ref[...] + jnp.dot(mixed, wo_ref[...], preferred_element_type=F32)
    h = _rms(x, g2_ref[...]).astype(BF16)
    mlp = None
    for c in range(D_FF // D_MODEL):
        cols = slice(c * D_MODEL, (c + 1) * D_MODEL)
        u = jnp.maximum(jnp.dot(h, wu_ref[:, cols], preferred_element_type=F32), 0.0)
        d = jnp.dot((u * u).astype(BF16), wd_ref[cols, :], preferred_element_type=F32)
        mlp = d if mlp is None else mlp + d
    x = x + mlp
    if final:
        x = _rms(x, gf_ref[...])
    y_ref[...] = x


def _mlp(final, x, oh, oa, wo, g2, wu, wd, gf):
    rows = x.shape[0]
    row_spec = lambda width: pl.BlockSpec((ROW_TILE, width), lambda i: (i, 0))
    return pl.pallas_call(
        functools.partial(_mlp_body, final),
        grid=(rows // ROW_TILE,),
        in_specs=[row_spec(D_MODEL), row_spec(HG_WIDTH), row_spec(ATT_WIDTH),
                  _resident((D_MODEL, D_MODEL)), _resident((1, D_MODEL)),
                  _resident((D_MODEL, D_FF)), _resident((D_FF, D_MODEL)), _resident((1, D_MODEL))],
        out_specs=row_spec(D_MODEL),
        out_shape=jax.ShapeDtypeStruct((rows, D_MODEL), F32),
        compiler_params=pltpu.CompilerParams(dimension_semantics=("parallel",),
                                             vmem_limit_bytes=VMEM_LIMIT),
        name="out_mlp",
    )(x, oh, oa, wo, g2, wu, wd, gf)


def _bias_by_distance(rel_bias_l, n_q, k_offset):
    left = n_q - 1 + k_offset - REL_CLIP
    right = EXT_LEN - (2 * REL_CLIP + 1) - left
    return jnp.pad(rel_bias_l.astype(F32), ((0, 0), (left, right)), mode="edge")


def kernel(x_prompt, x_sample, state_hgrn, cache_k, cache_v, lb_param, norm1_g, w_in, hg_norm_g,
           rel_bias, att_norm_g, w_out, norm2_g, w_up, w_down, final_norm_g):
    batch, seq, _ = x_prompt.shape
    dbatch, dseq, _ = x_sample.shape
    past = cache_k.shape[2]
    assert seq % ROW_TILE == 0 and min(ATT_REACH, seq) == ROW_TILE
    assert dbatch * dseq == ROW_TILE and dseq <= ATT_REACH
    xp = x_prompt.reshape(batch * seq, D_MODEL)
    xs = x_sample.reshape(dbatch * dseq, D_MODEL)
    ck = cache_k.reshape(DEPTH, dbatch, past, ATT_WIDTH)
    cv = cache_v.reshape(DEPTH, dbatch, past, ATT_WIDTH)
    zero_state = jnp.zeros((batch, HG_HEADS, HG_HEAD_DIM, HG_HEAD_DIM), F32)
    gf = final_norm_g.reshape(1, D_MODEL)
    row = lambda a, l: a[l].reshape(1, -1)
    pad = ((0, 0), (ATT_REACH, 0), (0, 0))

    sp_l, kp_l, vp_l, ss_l, ks_l, vs_l = [], [], [], [], [], []
    for l in range(DEPTH):
        w_in_l = w_in[l].astype(BF16)
        w_out_l = w_out[l].astype(BF16)
        w_up_l = w_up[l].astype(BF16)
        w_down_l = w_down[l].astype(BF16)
        g1, gh, ga, g2 = row(norm1_g, l), row(hg_norm_g, l), row(att_norm_g, l), row(norm2_g, l)
        final = l == DEPTH - 1

        hg, aq, ak, av, kf, vf = _inproj(l, xp, g1, w_in_l, lb_param, seq // ROW_TILE)
        oh, sp = _hgrn(hg, zero_state, gh, batch, seq)
        kpad = jnp.pad(ak.reshape(batch, seq, ATT_WIDTH), pad)
        vpad = jnp.pad(av.reshape(batch, seq, ATT_WIDTH), pad)
        ext_p = _bias_by_distance(rel_bias[l], Q_BLOCK, ATT_REACH)
        oa = _attn_prompt(aq, kpad, vpad, ext_p, ga, batch, seq)
        xp = _mlp(final, xp, oh, oa, w_out_l, g2, w_up_l, w_down_l, gf)
        sp_l.append(sp)
        kp_l.append(kf.reshape(batch, ROW_TILE, ATT_HEADS, ATT_HEAD_DIM))
        vp_l.append(vf.reshape(batch, ROW_TILE, ATT_HEADS, ATT_HEAD_DIM))

        hg, aq, ak, av, kf, vf = _inproj(l, xs, g1, w_in_l, lb_param, 1)
        oh, ss = _hgrn(hg, state_hgrn[l], gh, dbatch, dseq)
        ext_s = _bias_by_distance(rel_bias[l], dseq, past)
        oa = _attn_sample(l, aq, ak, av, ck, cv, ext_s, ga, dbatch, dseq)
        xs = _mlp(final, xs, oh, oa, w_out_l, g2, w_up_l, w_down_l, gf)
        ss_l.append(ss)
        ks_l.append(kf.reshape(dbatch, dseq, ATT_HEADS, ATT_HEAD_DIM))
        vs_l.append(vf.reshape(dbatch, dseq, ATT_HEADS, ATT_HEAD_DIM))

    return (xp.reshape(batch, seq, D_MODEL), xs.reshape(dbatch, dseq, D_MODEL),
            jnp.stack(sp_l), jnp.stack(kp_l), jnp.stack(vp_l),
            jnp.stack(ss_l), jnp.stack(ks_l), jnp.stack(vs_l))
```

```python
import functools
import math

import jax
import jax.numpy as jnp
from jax import lax
from jax.experimental import pallas as pl
from jax.experimental.pallas import tpu as pltpu

F32 = jnp.float32
BF16 = jnp.bfloat16

D_MODEL = 1024
DEPTH = 2
CHUNK = 64
HG_WIDTH = 512
HG_HEAD_DIM = 128
HG_HEADS = 4
ATT_WIDTH = 512
ATT_HEAD_DIM = 64
ATT_HEADS = 8
LEFT_CHUNKS = 8
ATT_REACH = LEFT_CHUNKS * CHUNK
REL_CLIP = 128
D_FF = 4 * D_MODEL
IN_WIDTH = 4 * HG_WIDTH + 3 * ATT_WIDTH
EPS = 1e-6
NEG = -1e30
LOG2E = math.log2(math.e)

LANES = 128
ROW_TILE = 512
Q_BLOCK = 256
KEY_WINDOW = ATT_REACH + Q_BLOCK
EXT_LEN = 1024
HG_COLS = 5 * HG_WIDTH
SAFE_EXP = 80.0
VMEM_LIMIT = 60 * 1024 * 1024


def _resident(shape):
    nd = len(shape)
    return pl.BlockSpec(shape, lambda *_: (0,) * nd, pipeline_mode=pl.Buffered(1))


def _rms(x, g):
    ms = jnp.mean(x * x, axis=-1, keepdims=True)
    return x * lax.rsqrt(ms + EPS) * g


def _inproj_body(layer, cache_period, x_ref, g_ref, w_ref, lbp_ref, hg_ref, aq_ref, ak_ref,
                 av_ref, kf_ref, vf_ref):
    h = _rms(x_ref[...], g_ref[...]).astype(BF16)

    def proj(sec):
        return jnp.dot(h, w_ref[:, sec * HG_WIDTH:(sec + 1) * HG_WIDTH],
                       preferred_element_type=F32)

    lp = lbp_ref[...]
    e = jnp.exp(lp - jnp.max(lp, axis=0, keepdims=True))
    sm = e / jnp.sum(e, axis=0, keepdims=True)
    lb = sm[0:1]
    for i in range(1, layer + 1):
        lb = lb + sm[i:i + 1]
    lb = lb - sm[0:1]

    f = lb + (1.0 - lb) * jax.nn.sigmoid(proj(1))
    hg_ref[:, 0 * HG_WIDTH:1 * HG_WIDTH] = proj(0)
    hg_ref[:, 1 * HG_WIDTH:2 * HG_WIDTH] = jnp.log(f)
    hg_ref[:, 2 * HG_WIDTH:3 * HG_WIDTH] = 1.0 - f
    hg_ref[:, 3 * HG_WIDTH:4 * HG_WIDTH] = proj(2)
    hg_ref[:, 4 * HG_WIDTH:5 * HG_WIDTH] = proj(3)
    aq_ref[...] = (proj(4) * (LOG2E / math.sqrt(ATT_HEAD_DIM))).astype(BF16)
    k = proj(5)
    v = proj(6)
    ak_ref[...] = k.astype(BF16)
    av_ref[...] = v.astype(BF16)

    @pl.when(pl.program_id(0) % cache_period == cache_period - 1)
    def _():
        kf_ref[...] = k
        vf_ref[...] = v


def _inproj(layer, x, g, w, lbp, cache_period):
    rows = x.shape[0]
    n_tiles = rows // ROW_TILE
    row_spec = lambda width: pl.BlockSpec((ROW_TILE, width), lambda i: (i, 0))
    cache_spec = pl.BlockSpec((ROW_TILE, ATT_WIDTH), lambda i: (i // cache_period, 0))
    cache_shape = jax.ShapeDtypeStruct((rows // cache_period, ATT_WIDTH), F32)
    return pl.pallas_call(
        functools.partial(_inproj_body, layer, cache_period),
        grid=(n_tiles,),
        in_specs=[row_spec(D_MODEL), _resident((1, D_MODEL)), _resident((D_MODEL, IN_WIDTH)),
                  _resident((DEPTH, HG_WIDTH))],
        out_specs=[row_spec(HG_COLS), row_spec(ATT_WIDTH), row_spec(ATT_WIDTH),
                   row_spec(ATT_WIDTH), cache_spec, cache_spec],
        out_shape=[jax.ShapeDtypeStruct((rows, HG_COLS), F32),
                   jax.ShapeDtypeStruct((rows, ATT_WIDTH), BF16),
                   jax.ShapeDtypeStruct((rows, ATT_WIDTH), BF16),
                   jax.ShapeDtypeStruct((rows, ATT_WIDTH), BF16),
                   cache_shape, cache_shape],
        compiler_params=pltpu.CompilerParams(dimension_semantics=("arbitrary",),
                                             vmem_limit_bytes=VMEM_LIMIT),
        name="inproj",
    )(x, g, w, lbp)


def _split3(x):
    hi = x.astype(BF16)
    r1 = x - hi.astype(F32)
    mid = r1.astype(BF16)
    lo = (r1 - mid.astype(F32)).astype(BF16)
    return hi, mid, lo


def _hgrn_body(chunk, n_chunks, hg_ref, s0_ref, gn_ref, o_ref, sout_ref, st_ref, sstart_ref):
    t = pl.program_id(1)
    mid = chunk // 2

    @pl.when(t == 0)
    def _():
        for h in range(HG_HEADS):
            st_ref[h] = s0_ref[0, h].T

    row_i = lax.broadcasted_iota(jnp.int32, (chunk, chunk), 0)
    col_i = lax.broadcasted_iota(jnp.int32, (chunk, chunk), 1)
    causal = row_i >= col_i
    tri = jnp.where(causal, 1.0, 0.0).astype(BF16)

    def direct_scores(q, k, b):
        lane = lax.broadcasted_iota(jnp.int32, (chunk, LANES), 1)

        def body(s, sc):
            sel = lax.broadcasted_iota(jnp.int32, (chunk, 1), 0) == s
            bs = jnp.sum(jnp.where(sel, b, 0.0), axis=0, keepdims=True)
            ks = jnp.sum(jnp.where(sel, k, 0.0), axis=0, keepdims=True)
            col = jnp.sum(q * ks * jnp.exp(jnp.minimum(b - bs, 0.0)), axis=-1, keepdims=True)
            return jnp.where(lane == s, col, sc)

        sc = lax.fori_loop(0, chunk, body, jnp.zeros((chunk, LANES), F32))
        return sc[:, :chunk]

    def run(fast):
        def chunk_step(c, worst):
            rows = pl.ds(pl.multiple_of(c * chunk, chunk), chunk)
            g_all = hg_ref[rows, 1 * HG_WIDTH:2 * HG_WIDTH]
            b_all = sum(jnp.dot(tri, part, preferred_element_type=F32) for part in _split3(g_all))
            for h in range(HG_HEADS):
                lanes = slice(h * HG_HEAD_DIM, (h + 1) * HG_HEAD_DIM)
                col = lambda sec: slice(sec * HG_WIDTH + h * HG_HEAD_DIM,
                                        sec * HG_WIDTH + (h + 1) * HG_HEAD_DIM)
                q = hg_ref[rows, col(0)]
                k = hg_ref[rows, col(2)]
                v = hg_ref[rows, col(3)].astype(BF16)
                gate = hg_ref[rows, col(4)]
                b = b_all[:, lanes]
                ref = b[mid - 1:mid]
                last = b[chunk - 1:chunk]
                st = st_ref[h]
                inter = lax.dot_general((q * jnp.exp(b)).astype(BF16), st.astype(BF16),
                                        (((1,), (1,)), ((), ())), preferred_element_type=F32)
                if fast:
                    qh = (q * jnp.exp(b - ref)).astype(BF16)
                    kh = (k * jnp.exp(ref - b)).astype(BF16)
                    sc = lax.dot_general(qh, kh, (((1,), (1,)), ((), ())),
                                         preferred_element_type=F32)
                    worst = jnp.maximum(worst, jnp.maximum(b[0:1] - ref, ref - last))
                else:
                    sc = direct_scores(q, k, b)
                sc = jnp.where(causal, sc, 0.0).astype(BF16)
                o = inter + jnp.dot(sc, v, preferred_element_type=F32)
                kd = (k * jnp.exp(last - b)).astype(BF16)
                st_ref[h] = st * jnp.exp(last) + lax.dot_general(
                    v, kd, (((0,), (0,)), ((), ())), preferred_element_type=F32)
                o = _rms(o, gn_ref[:, lanes]) * (gate * jax.nn.sigmoid(gate))
                o_ref[rows, lanes] = o.astype(o_ref.dtype)
            return worst

        return lax.fori_loop(0, n_chunks, chunk_step, jnp.zeros((1, LANES), F32),
                             unroll=n_chunks if fast else 1)

    sstart_ref[...] = st_ref[...]
    worst = run(fast=True)

    @pl.when(jnp.max(worst) > SAFE_EXP)
    def _():
        st_ref[...] = sstart_ref[...]
        run(fast=False)

    @pl.when(t == pl.num_programs(1) - 1)
    def _():
        for h in range(HG_HEADS):
            sout_ref[0, h] = st_ref[h].T


def _hgrn(hg, s0, gn, batch, seq):
    chunk = min(CHUNK, seq)
    block = min(ROW_TILE, seq)
    n_blocks = seq // block
    state_spec = pl.BlockSpec((1, HG_HEADS, HG_HEAD_DIM, HG_HEAD_DIM), lambda b, t: (b, 0, 0, 0))
    return pl.pallas_call(
        functools.partial(_hgrn_body, chunk, block // chunk),
        grid=(batch, n_blocks),
        in_specs=[pl.BlockSpec((block, HG_COLS), lambda b, t: (b * n_blocks + t, 0)),
                  state_spec, _resident((1, HG_WIDTH))],
        out_specs=[pl.BlockSpec((block, HG_WIDTH), lambda b, t: (b * n_blocks + t, 0)),
                   state_spec],
        out_shape=[jax.ShapeDtypeStruct((batch * seq, HG_WIDTH), BF16),
                   jax.ShapeDtypeStruct((batch, HG_HEADS, HG_HEAD_DIM, HG_HEAD_DIM), F32)],
        scratch_shapes=[pltpu.VMEM((HG_HEADS, HG_HEAD_DIM, HG_HEAD_DIM), F32),
                        pltpu.VMEM((HG_HEADS, HG_HEAD_DIM, HG_HEAD_DIM), F32)],
        compiler_params=pltpu.CompilerParams(dimension_semantics=("parallel", "arbitrary"),
                                             vmem_limit_bytes=VMEM_LIMIT),
        name="hgrn_scan",
    )(hg, s0, gn)


def _softmax_pv(s, v_masked):
    m = jnp.max(s, axis=-1, keepdims=True)
    p = jnp.exp2(s - m)
    l = jnp.sum(p, axis=-1, keepdims=True)
    return jnp.dot(p.astype(BF16), v_masked, preferred_element_type=F32) / l


def _toeplitz(ext_row, n_q):
    ext = jnp.broadcast_to(ext_row, (n_q, EXT_LEN))
    return pltpu.roll(ext, EXT_LEN - n_q + 1, 1, stride=1, stride_axis=0)


def _reduce_rows(x, combine, finish, slab=CHUNK):
    acc = x[0:slab]
    for i in range(1, x.shape[0] // slab):
        acc = combine(acc, x[i * slab:(i + 1) * slab])
    return finish(acc, axis=0, keepdims=True)


def _attn_prompt_body(q_ref, k_ref, v_ref, ext_ref, gn_ref, o_ref, bias_ref):
    qb = pl.program_id(1)
    start = pl.multiple_of(qb * Q_BLOCK, Q_BLOCK)

    @pl.when(qb == 0)
    def _():
        q_chunk = lax.broadcasted_iota(jnp.int32, (Q_BLOCK, KEY_WINDOW), 0) // CHUNK
        k_chunk = lax.broadcasted_iota(jnp.int32, (Q_BLOCK, KEY_WINDOW), 1) // CHUNK
        band = (k_chunk >= q_chunk) & (k_chunk <= q_chunk + LEFT_CHUNKS)
        for h in range(ATT_HEADS):
            t = _toeplitz(ext_ref[h:h + 1, :], Q_BLOCK)[:, :KEY_WINDOW] * LOG2E
            bias_ref[h] = jnp.where(band, t, NEG).T

    def block(mask_prefix):
        lane = lax.broadcasted_iota(jnp.int32, (1, LANES), 1)
        if mask_prefix:
            k_row = lax.broadcasted_iota(jnp.int32, (KEY_WINDOW, Q_BLOCK), 0)
            valid = k_row + start >= ATT_REACH
        def pair_scores(pair):
            lanes = slice(pair * LANES, (pair + 1) * LANES)
            qp = q_ref[:, lanes]
            kp = k_ref[0, pl.ds(start, KEY_WINDOW), lanes]
            qq = jnp.concatenate([jnp.where((lane // ATT_HEAD_DIM) == sub, qp, 0)
                                  for sub in range(2)], axis=0)
            return lax.dot_general(kp, qq, (((1,), (1,)), ((), ())),
                                   preferred_element_type=F32)

        outs = []
        n_pairs = ATT_HEADS // 2
        s_next = pair_scores(0)
        for pair in range(n_pairs):
            lanes = slice(pair * LANES, (pair + 1) * LANES)
            s_pair = s_next
            if pair + 1 < n_pairs:
                s_next = pair_scores(pair + 1)
            vt = v_ref[0, pl.ds(start, KEY_WINDOW), lanes].T
            for sub in range(2):
                s = s_pair[:, sub * Q_BLOCK:(sub + 1) * Q_BLOCK] + bias_ref[2 * pair + sub]
                if mask_prefix:
                    s = jnp.where(valid, s, NEG)
                p = jnp.exp2(s - _reduce_rows(s, jnp.maximum, jnp.max))
                l = _reduce_rows(p, jnp.add, jnp.sum)
                o = jnp.dot(vt[sub * ATT_HEAD_DIM:(sub + 1) * ATT_HEAD_DIM], p.astype(BF16),
                            preferred_element_type=F32)
                outs.append(o / l)
        o = jnp.concatenate(outs, axis=0).T
        o_ref[...] = _rms(o, gn_ref[...]).astype(o_ref.dtype)

    n_prefix = ATT_REACH // Q_BLOCK

    @pl.when(qb < n_prefix)
    def _():
        block(True)

    @pl.when(qb >= n_prefix)
    def _():
        block(False)


def _attn_prompt(q, kpad, vpad, ext, gn, batch, seq):
    n_q = seq // Q_BLOCK
    kv_spec = pl.BlockSpec((1, seq + ATT_REACH, ATT_WIDTH), lambda b, i: (b, 0, 0))
    return pl.pallas_call(
        _attn_prompt_body,
        grid=(batch, n_q),
        in_specs=[pl.BlockSpec((Q_BLOCK, ATT_WIDTH), lambda b, i: (b * n_q + i, 0)),
                  kv_spec, kv_spec, _resident((ATT_HEADS, EXT_LEN)), _resident((1, ATT_WIDTH))],
        out_specs=pl.BlockSpec((Q_BLOCK, ATT_WIDTH), lambda b, i: (b * n_q + i, 0)),
        out_shape=jax.ShapeDtypeStruct((batch * seq, ATT_WIDTH), BF16),
        scratch_shapes=[pltpu.VMEM((ATT_HEADS, KEY_WINDOW, Q_BLOCK), F32)],
        compiler_params=pltpu.CompilerParams(dimension_semantics=("parallel", "arbitrary"),
                                             vmem_limit_bytes=VMEM_LIMIT),
        name="attn_prompt",
    )(q, kpad, vpad, ext, gn)


def _attn_sample_body(q_ref, k_ref, v_ref, ck_ref, cv_ref, ext_ref, gn_ref, o_ref):
    n_q = q_ref.shape[0]
    n_k = ck_ref.shape[2] + n_q
    lane = lax.broadcasted_iota(jnp.int32, (1, LANES), 1)
    outs = []
    for pair in range(ATT_HEADS // 2):
        lanes = slice(pair * LANES, (pair + 1) * LANES)
        qp = q_ref[:, lanes]
        kp = jnp.concatenate([ck_ref[0, 0, :, lanes].astype(BF16), k_ref[:, lanes]], axis=0)
        vp = jnp.concatenate([cv_ref[0, 0, :, lanes].astype(BF16), v_ref[:, lanes]], axis=0)
        acc = jnp.zeros((n_q, LANES), F32)
        for sub in range(2):
            h = 2 * pair + sub
            mine = (lane // ATT_HEAD_DIM) == sub
            s = lax.dot_general(jnp.where(mine, qp, 0), kp, (((1,), (1,)), ((), ())),
                                preferred_element_type=F32)
            bias = _toeplitz(ext_ref[h:h + 1, :], n_q)[:, :n_k] * LOG2E
            acc = acc + _softmax_pv(s + bias, jnp.where(mine, vp, 0))
        outs.append(acc)
    o = jnp.concatenate(outs, axis=-1)
    o_ref[...] = _rms(o, gn_ref[...]).astype(o_ref.dtype)


def _attn_sample(layer, q, k, v, cache_k, cache_v, ext, gn, batch, seq):
    past = cache_k.shape[2]
    row_spec = pl.BlockSpec((seq, ATT_WIDTH), lambda b: (b, 0))
    cache_spec = pl.BlockSpec((1, 1, past, ATT_WIDTH), lambda b: (layer, b, 0, 0))
    return pl.pallas_call(
        _attn_sample_body,
        grid=(batch,),
        in_specs=[row_spec, row_spec, row_spec, cache_spec, cache_spec,
                  _resident((ATT_HEADS, EXT_LEN)), _resident((1, ATT_WIDTH))],
        out_specs=row_spec,
        out_shape=jax.ShapeDtypeStruct((batch * seq, ATT_WIDTH), BF16),
        compiler_params=pltpu.CompilerParams(dimension_semantics=("parallel",),
                                             vmem_limit_bytes=VMEM_LIMIT),
        name="attn_sample",
    )(q, k, v, cache_k, cache_v, ext, gn)


def _mlp_body(final, x_ref, oh_ref, oa_ref, wo_ref, g2_ref, wu_ref, wd_ref, gf_ref, y_ref):
    mixed = jnp.concatenate([oh_ref[...], oa_ref[...]], axis=-1)
    x = x_ref[...] + jnp.dot(mixed, wo_ref[...], preferred_element_type=F32)
    h = _rms(x, g2_ref[...]).astype(BF16)
    mlp = None
    for c in range(D_FF // D_MODEL):
        cols = slice(c * D_MODEL, (c + 1) * D_MODEL)
        u = jnp.maximum(jnp.dot(h, wu_ref[:, cols], preferred_element_type=F32), 0.0)
        d = jnp.dot((u * u).astype(BF16), wd_ref[cols, :], preferred_element_type=F32)
        mlp = d if mlp is None else mlp + d
    x = x + mlp
    if final:
        x = _rms(x, gf_ref[...])
    y_ref[...] = x


def _mlp(final, x, oh, oa, wo, g2, wu, wd, gf):
    rows = x.shape[0]
    row_spec = lambda width: pl.BlockSpec((ROW_TILE, width), lambda i: (i, 0))
    return pl.pallas_call(
        functools.partial(_mlp_body, final),
        grid=(rows // ROW_TILE,),
        in_specs=[row_spec(D_MODEL), row_spec(HG_WIDTH), row_spec(ATT_WIDTH),
                  _resident((D_MODEL, D_MODEL)), _resident((1, D_MODEL)),
                  _resident((D_MODEL, D_FF)), _resident((D_FF, D_MODEL)), _resident((1, D_MODEL))],
        out_specs=row_spec(D_MODEL),
        out_shape=jax.ShapeDtypeStruct((rows, D_MODEL), F32),
        compiler_params=pltpu.CompilerParams(dimension_semantics=("parallel",),
                                             vmem_limit_bytes=VMEM_LIMIT),
        name="out_mlp",
    )(x, oh, oa, wo, g2, wu, wd, gf)


def _bias_by_distance(rel_bias_l, n_q, k_offset):
    left = n_q - 1 + k_offset - REL_CLIP
    right = EXT_LEN - (2 * REL_CLIP + 1) - left
    return jnp.pad(rel_bias_l.astype(F32), ((0, 0), (left, right)), mode="edge")


def kernel(x_prompt, x_sample, state_hgrn, cache_k, cache_v, lb_param, norm1_g, w_in, hg_norm_g,
           rel_bias, att_norm_g, w_out, norm2_g, w_up, w_down, final_norm_g):
    batch, seq, _ = x_prompt.shape
    dbatch, dseq, _ = x_sample.shape
    past = cache_k.shape[2]
    assert seq % ROW_TILE == 0 and min(ATT_REACH, seq) == ROW_TILE
    assert dbatch * dseq == ROW_TILE and dseq <= ATT_REACH
    xp = x_prompt.reshape(batch * seq, D_MODEL)
    xs = x_sample.reshape(dbatch * dseq, D_MODEL)
    ck = cache_k.reshape(DEPTH, dbatch, past, ATT_WIDTH)
    cv = cache_v.reshape(DEPTH, dbatch, past, ATT_WIDTH)
    zero_state = jnp.zeros((batch, HG_HEADS, HG_HEAD_DIM, HG_HEAD_DIM), F32)
    gf = final_norm_g.reshape(1, D_MODEL)
    row = lambda a, l: a[l].reshape(1, -1)
    pad = ((0, 0), (ATT_REACH, 0), (0, 0))

    sp_l, kp_l, vp_l, ss_l, ks_l, vs_l = [], [], [], [], [], []
    for l in range(DEPTH):
        w_in_l = w_in[l].astype(BF16)
        w_out_l = w_out[l].astype(BF16)
        w_up_l = w_up[l].astype(BF16)
        w_down_l = w_down[l].astype(BF16)
        g1, gh, ga, g2 = row(norm1_g, l), row(hg_norm_g, l), row(att_norm_g, l), row(norm2_g, l)
        final = l == DEPTH - 1

        hg, aq, ak, av, kf, vf = _inproj(l, xp, g1, w_in_l, lb_param, seq // ROW_TILE)
        oh, sp = _hgrn(hg, zero_state, gh, batch, seq)
        kpad = jnp.pad(ak.reshape(batch, seq, ATT_WIDTH), pad)
        vpad = jnp.pad(av.reshape(batch, seq, ATT_WIDTH), pad)
        ext_p = _bias_by_distance(rel_bias[l], Q_BLOCK, ATT_REACH)
        oa = _attn_prompt(aq, kpad, vpad, ext_p, ga, batch, seq)
        xp = _mlp(final, xp, oh, oa, w_out_l, g2, w_up_l, w_down_l, gf)
        sp_l.append(sp)
        kp_l.append(kf.reshape(batch, ROW_TILE, ATT_HEADS, ATT_HEAD_DIM))
        vp_l.append(vf.reshape(batch, ROW_TILE, ATT_HEADS, ATT_HEAD_DIM))

        hg, aq, ak, av, kf, vf = _inproj(l, xs, g1, w_in_l, lb_param, 1)
        oh, ss = _hgrn(hg, state_hgrn[l], gh, dbatch, dseq)
        ext_s = _bias_by_distance(rel_bias[l], dseq, past)
        oa = _attn_sample(l, aq, ak, av, ck, cv, ext_s, ga, dbatch, dseq)
        xs = _mlp(final, xs, oh, oa, w_out_l, g2, w_up_l, w_down_l, gf)
        ss_l.append(ss)
        ks_l.append(kf.reshape(dbatch, dseq, ATT_HEADS, ATT_HEAD_DIM))
        vs_l.append(vf.reshape(dbatch, dseq, ATT_HEADS, ATT_HEAD_DIM))

    return (xp.reshape(batch, seq, D_MODEL), xs.reshape(dbatch, dseq, D_MODEL),
            jnp.stack(sp_l), jnp.stack(kp_l), jnp.stack(vp_l),
            jnp.stack(ss_l), jnp.stack(ks_l), jnp.stack(vs_l))
```

```python
import functools
import math

import jax
import jax.numpy as jnp
from jax import lax
from jax.experimental import pallas as pl
from jax.experimental.pallas import tpu as pltpu

F32 = jnp.float32
BF16 = jnp.bfloat16

D_MODEL = 1024
DEPTH = 2
CHUNK = 64
HG_WIDTH = 512
HG_HEAD_DIM = 128
HG_HEADS = 4
ATT_WIDTH = 512
ATT_HEAD_DIM = 64
ATT_HEADS = 8
LEFT_CHUNKS = 8
ATT_REACH = LEFT_CHUNKS * CHUNK
REL_CLIP = 128
D_FF = 4 * D_MODEL
IN_WIDTH = 4 * HG_WIDTH + 3 * ATT_WIDTH
EPS = 1e-6
NEG = -1e30
LOG2E = math.log2(math.e)

LANES = 128
ROW_TILE = 512
Q_BLOCK = 256
KEY_WINDOW = ATT_REACH + Q_BLOCK
EXT_LEN = 1024
HG_COLS = 5 * HG_WIDTH
SAFE_EXP = 80.0
VMEM_LIMIT = 60 * 1024 * 1024


def _resident(shape):
    nd = len(shape)
    return pl.BlockSpec(shape, lambda *_: (0,) * nd, pipeline_mode=pl.Buffered(1))


def _rms(x, g):
    ms = jnp.mean(x * x, axis=-1, keepdims=True)
    return x * lax.rsqrt(ms + EPS) * g


def _inproj_body(layer, cache_period, x_ref, g_ref, w_ref, lbp_ref, hg_ref, aq_ref, ak_ref,
                 av_ref, kf_ref, vf_ref):
    h = _rms(x_ref[...], g_ref[...]).astype(BF16)

    def proj(sec):
        return jnp.dot(h, w_ref[:, sec * HG_WIDTH:(sec + 1) * HG_WIDTH],
                       preferred_element_type=F32)

    lp = lbp_ref[...]
    e = jnp.exp(lp - jnp.max(lp, axis=0, keepdims=True))
    sm = e / jnp.sum(e, axis=0, keepdims=True)
    lb = sm[0:1]
    for i in range(1, layer + 1):
        lb = lb + sm[i:i + 1]
    lb = lb - sm[0:1]

    f = lb + (1.0 - lb) * jax.nn.sigmoid(proj(1))
    hg_ref[:, 0 * HG_WIDTH:1 * HG_WIDTH] = proj(0)
    hg_ref[:, 1 * HG_WIDTH:2 * HG_WIDTH] = jnp.log(f)
    hg_ref[:, 2 * HG_WIDTH:3 * HG_WIDTH] = 1.0 - f
    hg_ref[:, 3 * HG_WIDTH:4 * HG_WIDTH] = proj(2)
    hg_ref[:, 4 * HG_WIDTH:5 * HG_WIDTH] = proj(3)
    aq_ref[...] = (proj(4) * (LOG2E / math.sqrt(ATT_HEAD_DIM))).astype(BF16)
    k = proj(5)
    v = proj(6)
    ak_ref[...] = k.astype(BF16)
    av_ref[...] = v.astype(BF16)

    @pl.when(pl.program_id(0) % cache_period == cache_period - 1)
    def _():
        kf_ref[...] = k
        vf_ref[...] = v


def _inproj(layer, x, g, w, lbp, cache_period):
    rows = x.shape[0]
    n_tiles = rows // ROW_TILE
    row_spec = lambda width: pl.BlockSpec((ROW_TILE, width), lambda i: (i, 0))
    cache_spec = pl.BlockSpec((ROW_TILE, ATT_WIDTH), lambda i: (i // cache_period, 0))
    cache_shape = jax.ShapeDtypeStruct((rows // cache_period, ATT_WIDTH), F32)
    return pl.pallas_call(
        functools.partial(_inproj_body, layer, cache_period),
        grid=(n_tiles,),
        in_specs=[row_spec(D_MODEL), _resident((1, D_MODEL)), _resident((D_MODEL, IN_WIDTH)),
                  _resident((DEPTH, HG_WIDTH))],
        out_specs=[row_spec(HG_COLS), row_spec(ATT_WIDTH), row_spec(ATT_WIDTH),
                   row_spec(ATT_WIDTH), cache_spec, cache_spec],
        out_shape=[jax.ShapeDtypeStruct((rows, HG_COLS), F32),
                   jax.ShapeDtypeStruct((rows, ATT_WIDTH), BF16),
                   jax.ShapeDtypeStruct((rows, ATT_WIDTH), BF16),
                   jax.ShapeDtypeStruct((rows, ATT_WIDTH), BF16),
                   cache_shape, cache_shape],
        compiler_params=pltpu.CompilerParams(dimension_semantics=("arbitrary",),
                                             vmem_limit_bytes=VMEM_LIMIT),
        name="inproj",
    )(x, g, w, lbp)


def _split3(x):
    hi = x.astype(BF16)
    r1 = x - hi.astype(F32)
    mid = r1.astype(BF16)
    lo = (r1 - mid.astype(F32)).astype(BF16)
    return hi, mid, lo


def _hgrn_body(chunk, n_chunks, hg_ref, s0_ref, gn_ref, o_ref, sout_ref, st_ref, sstart_ref):
    t = pl.program_id(1)
    mid = chunk // 2

    @pl.when(t == 0)
    def _():
        for h in range(HG_HEADS):
            st_ref[h] = s0_ref[0, h].T

    row_i = lax.broadcasted_iota(jnp.int32, (chunk, chunk), 0)
    col_i = lax.broadcasted_iota(jnp.int32, (chunk, chunk), 1)
    causal = row_i >= col_i
    tri = jnp.where(causal, 1.0, 0.0).astype(BF16)

    def direct_scores(q, k, b):
        lane = lax.broadcasted_iota(jnp.int32, (chunk, LANES), 1)

        def body(s, sc):
            sel = lax.broadcasted_iota(jnp.int32, (chunk, 1), 0) == s
            bs = jnp.sum(jnp.where(sel, b, 0.0), axis=0, keepdims=True)
            ks = jnp.sum(jnp.where(sel, k, 0.0), axis=0, keepdims=True)
            col = jnp.sum(q * ks * jnp.exp(jnp.minimum(b - bs, 0.0)), axis=-1, keepdims=True)
            return jnp.where(lane == s, col, sc)

        sc = lax.fori_loop(0, chunk, body, jnp.zeros((chunk, LANES), F32))
        return sc[:, :chunk]

    def head_cols(h, sec):
        return slice(sec * HG_WIDTH + h * HG_HEAD_DIM, sec * HG_WIDTH + (h + 1) * HG_HEAD_DIM)

    def finish_head(rows, h, st, qe, sc, kd, decay):
        lanes = slice(h * HG_HEAD_DIM, (h + 1) * HG_HEAD_DIM)
        v = hg_ref[rows, head_cols(h, 3)].astype(BF16)
        gate = hg_ref[rows, head_cols(h, 4)]
        inter = lax.dot_general(qe, st.astype(BF16), (((1,), (1,)), ((), ())),
                                preferred_element_type=F32)
        sc = jnp.where(causal, sc, 0.0).astype(BF16)
        o = inter + jnp.dot(sc, v, preferred_element_type=F32)
        st_ref[h] = st * decay + lax.dot_general(v, kd, (((0,), (0,)), ((), ())),
                                                 preferred_element_type=F32)
        o = _rms(o, gn_ref[:, lanes]) * (gate * jax.nn.sigmoid(gate))
        o_ref[rows, lanes] = o.astype(o_ref.dtype)

    def cumulative_decay(rows):
        g_all = hg_ref[rows, 1 * HG_WIDTH:2 * HG_WIDTH]
        return sum(jnp.dot(tri, part, preferred_element_type=F32) for part in _split3(g_all))

    def decayed_operands(c):
        rows = slice(c * chunk, (c + 1) * chunk)
        b = cumulative_decay(rows)
        q = hg_ref[rows, 0:HG_WIDTH]
        k = hg_ref[rows, 2 * HG_WIDTH:3 * HG_WIDTH]
        ref = b[mid - 1:mid]
        last = b[chunk - 1:chunk]
        return ((q * jnp.exp(b)).astype(BF16), (q * jnp.exp(b - ref)).astype(BF16),
                (k * jnp.exp(ref - b)).astype(BF16), (k * jnp.exp(last - b)).astype(BF16),
                jnp.exp(last), jnp.maximum(b[0:1] - ref, ref - last))

    def run_fast():
        worst = jnp.zeros((1, HG_WIDTH), F32)
        ahead = decayed_operands(0)
        for c in range(n_chunks):
            qe, qh, kh, kd, decay, w = ahead
            if c + 1 < n_chunks:
                ahead = decayed_operands(c + 1)
            worst = jnp.maximum(worst, w)
            rows = slice(c * chunk, (c + 1) * chunk)
            for h in range(HG_HEADS):
                lanes = slice(h * HG_HEAD_DIM, (h + 1) * HG_HEAD_DIM)
                sc = lax.dot_general(qh[:, lanes], kh[:, lanes], (((1,), (1,)), ((), ())),
                                     preferred_element_type=F32)
                finish_head(rows, h, st_ref[h], qe[:, lanes], sc, kd[:, lanes], decay[:, lanes])
        return worst

    def run_exact():
        def chunk_step(c, carry):
            rows = pl.ds(pl.multiple_of(c * chunk, chunk), chunk)
            b_all = cumulative_decay(rows)
            for h in range(HG_HEADS):
                lanes = slice(h * HG_HEAD_DIM, (h + 1) * HG_HEAD_DIM)
                q = hg_ref[rows, head_cols(h, 0)]
                k = hg_ref[rows, head_cols(h, 2)]
                b = b_all[:, lanes]
                last = b[chunk - 1:chunk]
                finish_head(rows, h, st_ref[h], (q * jnp.exp(b)).astype(BF16),
                            direct_scores(q, k, b), (k * jnp.exp(last - b)).astype(BF16),
                            jnp.exp(last))
            return carry

        lax.fori_loop(0, n_chunks, chunk_step, 0)

    sstart_ref[...] = st_ref[...]
    worst = run_fast()

    @pl.when(jnp.max(worst) > SAFE_EXP)
    def _():
        st_ref[...] = sstart_ref[...]
        run_exact()

    @pl.when(t == pl.num_programs(1) - 1)
    def _():
        for h in range(HG_HEADS):
            sout_ref[0, h] = st_ref[h].T


def _hgrn(hg, s0, gn, batch, seq):
    chunk = min(CHUNK, seq)
    block = min(ROW_TILE, seq)
    n_blocks = seq // block
    state_spec = pl.BlockSpec((1, HG_HEADS, HG_HEAD_DIM, HG_HEAD_DIM), lambda b, t: (b, 0, 0, 0))
    return pl.pallas_call(
        functools.partial(_hgrn_body, chunk, block // chunk),
        grid=(batch, n_blocks),
        in_specs=[pl.BlockSpec((block, HG_COLS), lambda b, t: (b * n_blocks + t, 0)),
                  state_spec, _resident((1, HG_WIDTH))],
        out_specs=[pl.BlockSpec((block, HG_WIDTH), lambda b, t: (b * n_blocks + t, 0)),
                   state_spec],
        out_shape=[jax.ShapeDtypeStruct((batch * seq, HG_WIDTH), BF16),
                   jax.ShapeDtypeStruct((batch, HG_HEADS, HG_HEAD_DIM, HG_HEAD_DIM), F32)],
        scratch_shapes=[pltpu.VMEM((HG_HEADS, HG_HEAD_DIM, HG_HEAD_DIM), F32),
                        pltpu.VMEM((HG_HEADS, HG_HEAD_DIM, HG_HEAD_DIM), F32)],
        compiler_params=pltpu.CompilerParams(dimension_semantics=("parallel", "arbitrary"),
                                             vmem_limit_bytes=VMEM_LIMIT),
        name="hgrn_scan",
    )(hg, s0, gn)


def _softmax_pv(s, v_masked):
    m = jnp.max(s, axis=-1, keepdims=True)
    p = jnp.exp2(s - m)
    l = jnp.sum(p, axis=-1, keepdims=True)
    return jnp.dot(p.astype(BF16), v_masked, preferred_element_type=F32) / l


def _toeplitz(ext_row, n_q):
    ext = jnp.broadcast_to(ext_row, (n_q, EXT_LEN))
    return pltpu.roll(ext, EXT_LEN - n_q + 1, 1, stride=1, stride_axis=0)


def _reduce_rows(x, combine, finish, slab=CHUNK):
    acc = x[0:slab]
    for i in range(1, x.shape[0] // slab):
        acc = combine(acc, x[i * slab:(i + 1) * slab])
    return finish(acc, axis=0, keepdims=True)


def _attn_prompt_body(q_ref, k_ref, v_ref, ext_ref, gn_ref, o_ref, bias_ref):
    qb = pl.program_id(1)
    start = pl.multiple_of(qb * Q_BLOCK, Q_BLOCK)

    @pl.when(qb == 0)
    def _():
        q_chunk = lax.broadcasted_iota(jnp.int32, (Q_BLOCK, KEY_WINDOW), 0) // CHUNK
        k_chunk = lax.broadcasted_iota(jnp.int32, (Q_BLOCK, KEY_WINDOW), 1) // CHUNK
        band = (k_chunk >= q_chunk) & (k_chunk <= q_chunk + LEFT_CHUNKS)
        for h in range(ATT_HEADS):
            t = _toeplitz(ext_ref[h:h + 1, :], Q_BLOCK)[:, :KEY_WINDOW] * LOG2E
            bias_ref[h] = jnp.where(band, t, NEG).T

    def block(mask_prefix):
        lane = lax.broadcasted_iota(jnp.int32, (1, LANES), 1)
        if mask_prefix:
            k_row = lax.broadcasted_iota(jnp.int32, (KEY_WINDOW, Q_BLOCK), 0)
            valid = k_row + start >= ATT_REACH
        def pair_scores(pair):
            lanes = slice(pair * LANES, (pair + 1) * LANES)
            qp = q_ref[:, lanes]
            kp = k_ref[0, pl.ds(start, KEY_WINDOW), lanes]
            qq = jnp.concatenate([jnp.where((lane // ATT_HEAD_DIM) == sub, qp, 0)
                                  for sub in range(2)], axis=0)
            return lax.dot_general(kp, qq, (((1,), (1,)), ((), ())),
                                   preferred_element_type=F32)

        outs = []
        n_pairs = ATT_HEADS // 2
        s_next = pair_scores(0)
        for pair in range(n_pairs):
            lanes = slice(pair * LANES, (pair + 1) * LANES)
            s_pair = s_next
            if pair + 1 < n_pairs:
                s_next = pair_scores(pair + 1)
            vt = v_ref[0, pl.ds(start, KEY_WINDOW), lanes].T
            for sub in range(2):
                s = s_pair[:, sub * Q_BLOCK:(sub + 1) * Q_BLOCK] + bias_ref[2 * pair + sub]
                if mask_prefix:
                    s = jnp.where(valid, s, NEG)
                p = jnp.exp2(s - _reduce_rows(s, jnp.maximum, jnp.max))
                l = _reduce_rows(p, jnp.add, jnp.sum)
                o = jnp.dot(vt[sub * ATT_HEAD_DIM:(sub + 1) * ATT_HEAD_DIM], p.astype(BF16),
                            preferred_element_type=F32)
                outs.append(o / l)
        o = jnp.concatenate(outs, axis=0).T
        o_ref[...] = _rms(o, gn_ref[...]).astype(o_ref.dtype)

    n_prefix = ATT_REACH // Q_BLOCK

    @pl.when(qb < n_prefix)
    def _():
        block(True)

    @pl.when(qb >= n_prefix)
    def _():
        block(False)


def _attn_prompt(q, kpad, vpad, ext, gn, batch, seq):
    n_q = seq // Q_BLOCK
    kv_spec = pl.BlockSpec((1, seq + ATT_REACH, ATT_WIDTH), lambda b, i: (b, 0, 0))
    return pl.pallas_call(
        _attn_prompt_body,
        grid=(batch, n_q),
        in_specs=[pl.BlockSpec((Q_BLOCK, ATT_WIDTH), lambda b, i: (b * n_q + i, 0)),
                  kv_spec, kv_spec, _resident((ATT_HEADS, EXT_LEN)), _resident((1, ATT_WIDTH))],
        out_specs=pl.BlockSpec((Q_BLOCK, ATT_WIDTH), lambda b, i: (b * n_q + i, 0)),
        out_shape=jax.ShapeDtypeStruct((batch * seq, ATT_WIDTH), BF16),
        scratch_shapes=[pltpu.VMEM((ATT_HEADS, KEY_WINDOW, Q_BLOCK), F32)],
        compiler_params=pltpu.CompilerParams(dimension_semantics=("parallel", "arbitrary"),
                                             vmem_limit_bytes=VMEM_LIMIT),
        name="attn_prompt",
    )(q, kpad, vpad, ext, gn)


def _attn_sample_body(q_ref, k_ref, v_ref, ck_ref, cv_ref, ext_ref, gn_ref, o_ref):
    n_q = q_ref.shape[0]
    n_k = ck_ref.shape[2] + n_q
    lane = lax.broadcasted_iota(jnp.int32, (1, LANES), 1)
    outs = []
    for pair in range(ATT_HEADS // 2):
        lanes = slice(pair * LANES, (pair + 1) * LANES)
        qp = q_ref[:, lanes]
        kp = jnp.concatenate([ck_ref[0, 0, :, lanes].astype(BF16), k_ref[:, lanes]], axis=0)
        vp = jnp.concatenate([cv_ref[0, 0, :, lanes].astype(BF16), v_ref[:, lanes]], axis=0)
        acc = jnp.zeros((n_q, LANES), F32)
        for sub in range(2):
            h = 2 * pair + sub
            mine = (lane // ATT_HEAD_DIM) == sub
            s = lax.dot_general(jnp.where(mine, qp, 0), kp, (((1,), (1,)), ((), ())),
                                preferred_element_type=F32)
            bias = _toeplitz(ext_ref[h:h + 1, :], n_q)[:, :n_k] * LOG2E
            acc = acc + _softmax_pv(s + bias, jnp.where(mine, vp, 0))
        outs.append(acc)
    o = jnp.concatenate(outs, axis=-1)
    o_ref[...] = _rms(o, gn_ref[...]).astype(o_ref.dtype)


def _attn_sample(layer, q, k, v, cache_k, cache_v, ext, gn, batch, seq):
    past = cache_k.shape[2]
    row_spec = pl.BlockSpec((seq, ATT_WIDTH), lambda b: (b, 0))
    cache_spec = pl.BlockSpec((1, 1, past, ATT_WIDTH), lambda b: (layer, b, 0, 0))
    return pl.pallas_call(
        _attn_sample_body,
        grid=(batch,),
        in_specs=[row_spec, row_spec, row_spec, cache_spec, cache_spec,
                  _resident((ATT_HEADS, EXT_LEN)), _resident((1, ATT_WIDTH))],
        out_specs=row_spec,
        out_shape=jax.ShapeDtypeStruct((batch * seq, ATT_WIDTH), BF16),
        compiler_params=pltpu.CompilerParams(dimension_semantics=("parallel",),
                                             vmem_limit_bytes=VMEM_LIMIT),
        name="attn_sample",
    )(q, k, v, cache_k, cache_v, ext, gn)


def _mlp_body(final, x_ref, oh_ref, oa_ref, wo_ref, g2_ref, wu_ref, wd_ref, gf_ref, y_ref):
    mixed = jnp.concatenate([oh_ref[...], oa_ref[...]], axis=-1)
    x = x_ref[...] + jnp.dot(mixed, wo_ref[...], preferred_element_type=F32)
    h = _rms(x, g2_ref[...]).astype(BF16)
    mlp = None
    for c in range(D_FF // D_MODEL):
        cols = slice(c * D_MODEL, (c + 1) * D_MODEL)
        u = jnp.maximum(jnp.dot(h, wu_ref[:, cols], preferred_element_type=F32), 0.0)
        d = jnp.dot((u * u).astype(BF16), wd_ref[cols, :], preferred_element_type=F32)
        mlp = d if mlp is None else mlp + d
    x = x + mlp
    if final:
        x = _rms(x, gf_ref[...])
    y_ref[...] = x


def _mlp(final, x, oh, oa, wo, g2, wu, wd, gf):
    rows = x.shape[0]
    row_spec = lambda width: pl.BlockSpec((ROW_TILE, width), lambda i: (i, 0))
    return pl.pallas_call(
        functools.partial(_mlp_body, final),
        grid=(rows // ROW_TILE,),
        in_specs=[row_spec(D_MODEL), row_spec(HG_WIDTH), row_spec(ATT_WIDTH),
                  _resident((D_MODEL, D_MODEL)), _resident((1, D_MODEL)),
                  _resident((D_MODEL, D_FF)), _resident((D_FF, D_MODEL)), _resident((1, D_MODEL))],
        out_specs=row_spec(D_MODEL),
        out_shape=jax.ShapeDtypeStruct((rows, D_MODEL), F32),
        compiler_params=pltpu.CompilerParams(dimension_semantics=("parallel",),
                                             vmem_limit_bytes=VMEM_LIMIT),
        name="out_mlp",
    )(x, oh, oa, wo, g2, wu, wd, gf)


def _bias_by_distance(rel_bias_l, n_q, k_offset):
    left = n_q - 1 + k_offset - REL_CLIP
    right = EXT_LEN - (2 * REL_CLIP + 1) - left
    return jnp.pad(rel_bias_l.astype(F32), ((0, 0), (left, right)), mode="edge")


def kernel(x_prompt, x_sample, state_hgrn, cache_k, cache_v, lb_param, norm1_g, w_in, hg_norm_g,
           rel_bias, att_norm_g, w_out, norm2_g, w_up, w_down, final_norm_g):
    batch, seq, _ = x_prompt.shape
    dbatch, dseq, _ = x_sample.shape
    past = cache_k.shape[2]
    assert seq % ROW_TILE == 0 and min(ATT_REACH, seq) == ROW_TILE
    assert dbatch * dseq == ROW_TILE and dseq <= ATT_REACH
    xp = x_prompt.reshape(batch * seq, D_MODEL)
    xs = x_sample.reshape(dbatch * dseq, D_MODEL)
    ck = cache_k.reshape(DEPTH, dbatch, past, ATT_WIDTH)
    cv = cache_v.reshape(DEPTH, dbatch, past, ATT_WIDTH)
    zero_state = jnp.zeros((batch, HG_HEADS, HG_HEAD_DIM, HG_HEAD_DIM), F32)
    gf = final_norm_g.reshape(1, D_MODEL)
    row = lambda a, l: a[l].reshape(1, -1)
    pad = ((0, 0), (ATT_REACH, 0), (0, 0))

    sp_l, kp_l, vp_l, ss_l, ks_l, vs_l = [], [], [], [], [], []
    for l in range(DEPTH):
        w_in_l = w_in[l].astype(BF16)
        w_out_l = w_out[l].astype(BF16)
        w_up_l = w_up[l].astype(BF16)
        w_down_l = w_down[l].astype(BF16)
        g1, gh, ga, g2 = row(norm1_g, l), row(hg_norm_g, l), row(att_norm_g, l), row(norm2_g, l)
        final = l == DEPTH - 1

        hg, aq, ak, av, kf, vf = _inproj(l, xp, g1, w_in_l, lb_param, seq // ROW_TILE)
        oh, sp = _hgrn(hg, zero_state, gh, batch, seq)
        kpad = jnp.pad(ak.reshape(batch, seq, ATT_WIDTH), pad)
        vpad = jnp.pad(av.reshape(batch, seq, ATT_WIDTH), pad)
        ext_p = _bias_by_distance(rel_bias[l], Q_BLOCK, ATT_REACH)
        oa = _attn_prompt(aq, kpad, vpad, ext_p, ga, batch, seq)
        xp = _mlp(final, xp, oh, oa, w_out_l, g2, w_up_l, w_down_l, gf)
        sp_l.append(sp)
        kp_l.append(kf.reshape(batch, ROW_TILE, ATT_HEADS, ATT_HEAD_DIM))
        vp_l.append(vf.reshape(batch, ROW_TILE, ATT_HEADS, ATT_HEAD_DIM))

        hg, aq, ak, av, kf, vf = _inproj(l, xs, g1, w_in_l, lb_param, 1)
        oh, ss = _hgrn(hg, state_hgrn[l], gh, dbatch, dseq)
        ext_s = _bias_by_distance(rel_bias[l], dseq, past)
        oa = _attn_sample(l, aq, ak, av, ck, cv, ext_s, ga, dbatch, dseq)
        xs = _mlp(final, xs, oh, oa, w_out_l, g2, w_up_l, w_down_l, gf)
        ss_l.append(ss)
        ks_l.append(kf.reshape(dbatch, dseq, ATT_HEADS, ATT_HEAD_DIM))
        vs_l.append(vf.reshape(dbatch, dseq, ATT_HEADS, ATT_HEAD_DIM))

    return (xp.reshape(batch, seq, D_MODEL), xs.reshape(dbatch, dseq, D_MODEL),
            jnp.stack(sp_l), jnp.stack(kp_l), jnp.stack(vp_l),
            jnp.stack(ss_l), jnp.stack(ks_l), jnp.stack(vs_l))
```

```python
import functools
import math

import jax
import jax.numpy as jnp
from jax import lax
from jax.experimental import pallas as pl
from jax.experimental.pallas import tpu as pltpu

F32 = jnp.float32
BF16 = jnp.bfloat16

D_MODEL = 1024
DEPTH = 2
CHUNK = 64
HG_WIDTH = 512
HG_HEAD_DIM = 128
HG_HEADS = 4
ATT_WIDTH = 512
ATT_HEAD_DIM = 64
ATT_HEADS = 8
LEFT_CHUNKS = 8
ATT_REACH = LEFT_CHUNKS * CHUNK
REL_CLIP = 128
D_FF = 4 * D_MODEL
IN_WIDTH = 4 * HG_WIDTH + 3 * ATT_WIDTH
EPS = 1e-6
NEG = -1e30
LOG2E = math.log2(math.e)

LANES = 128
ROW_TILE = 512
Q_BLOCK = 256
KEY_WINDOW = ATT_REACH + Q_BLOCK
GROUP_KEYS = ATT_REACH + LANES
EXT_LEN = 1024
HG_COLS = 5 * HG_WIDTH
SAFE_EXP = 80.0
VMEM_LIMIT = 60 * 1024 * 1024


def _resident(shape):
    nd = len(shape)
    return pl.BlockSpec(shape, lambda *_: (0,) * nd, pipeline_mode=pl.Buffered(1))


def _rms(x, g):
    ms = jnp.mean(x * x, axis=-1, keepdims=True)
    return x * lax.rsqrt(ms + EPS) * g


def _inproj_body(layer, cache_period, x_ref, g_ref, w_ref, lbp_ref, hg_ref, aq_ref, ak_ref,
                 av_ref, kf_ref, vf_ref):
    h = _rms(x_ref[...], g_ref[...]).astype(BF16)

    def proj(sec):
        return jnp.dot(h, w_ref[:, sec * HG_WIDTH:(sec + 1) * HG_WIDTH],
                       preferred_element_type=F32)

    lp = lbp_ref[...]
    e = jnp.exp(lp - jnp.max(lp, axis=0, keepdims=True))
    sm = e / jnp.sum(e, axis=0, keepdims=True)
    lb = sm[0:1]
    for i in range(1, layer + 1):
        lb = lb + sm[i:i + 1]
    lb = lb - sm[0:1]

    f = lb + (1.0 - lb) * jax.nn.sigmoid(proj(1))
    hg_ref[:, 0 * HG_WIDTH:1 * HG_WIDTH] = proj(0)
    hg_ref[:, 1 * HG_WIDTH:2 * HG_WIDTH] = jnp.log(f)
    hg_ref[:, 2 * HG_WIDTH:3 * HG_WIDTH] = 1.0 - f
    hg_ref[:, 3 * HG_WIDTH:4 * HG_WIDTH] = proj(2)
    hg_ref[:, 4 * HG_WIDTH:5 * HG_WIDTH] = proj(3)
    aq_ref[...] = (proj(4) * (LOG2E / math.sqrt(ATT_HEAD_DIM))).astype(BF16)
    k = proj(5)
    v = proj(6)
    ak_ref[...] = k.astype(BF16)
    av_ref[...] = v.astype(BF16)

    @pl.when(pl.program_id(0) % cache_period == cache_period - 1)
    def _():
        kf_ref[...] = k
        vf_ref[...] = v


def _inproj(layer, x, g, w, lbp, cache_period):
    rows = x.shape[0]
    n_tiles = rows // ROW_TILE
    row_spec = lambda width: pl.BlockSpec((ROW_TILE, width), lambda i: (i, 0))
    cache_spec = pl.BlockSpec((ROW_TILE, ATT_WIDTH), lambda i: (i // cache_period, 0))
    cache_shape = jax.ShapeDtypeStruct((rows // cache_period, ATT_WIDTH), F32)
    return pl.pallas_call(
        functools.partial(_inproj_body, layer, cache_period),
        grid=(n_tiles,),
        in_specs=[row_spec(D_MODEL), _resident((1, D_MODEL)), _resident((D_MODEL, IN_WIDTH)),
                  _resident((DEPTH, HG_WIDTH))],
        out_specs=[row_spec(HG_COLS), row_spec(ATT_WIDTH), row_spec(ATT_WIDTH),
                   row_spec(ATT_WIDTH), cache_spec, cache_spec],
        out_shape=[jax.ShapeDtypeStruct((rows, HG_COLS), F32),
                   jax.ShapeDtypeStruct((rows, ATT_WIDTH), BF16),
                   jax.ShapeDtypeStruct((rows, ATT_WIDTH), BF16),
                   jax.ShapeDtypeStruct((rows, ATT_WIDTH), BF16),
                   cache_shape, cache_shape],
        compiler_params=pltpu.CompilerParams(dimension_semantics=("arbitrary",),
                                             vmem_limit_bytes=VMEM_LIMIT),
        name="inproj",
    )(x, g, w, lbp)


def _split3(x):
    hi = x.astype(BF16)
    r1 = x - hi.astype(F32)
    mid = r1.astype(BF16)
    lo = (r1 - mid.astype(F32)).astype(BF16)
    return hi, mid, lo


def _hgrn_body(chunk, n_chunks, hg_ref, s0_ref, gn_ref, o_ref, sout_ref, st_ref, sstart_ref):
    t = pl.program_id(1)
    mid = chunk // 2

    @pl.when(t == 0)
    def _():
        for h in range(HG_HEADS):
            st_ref[h] = s0_ref[0, h].T

    row_i = lax.broadcasted_iota(jnp.int32, (chunk, chunk), 0)
    col_i = lax.broadcasted_iota(jnp.int32, (chunk, chunk), 1)
    causal = row_i >= col_i
    tri = jnp.where(causal, 1.0, 0.0).astype(BF16)

    def direct_scores(q, k, b):
        lane = lax.broadcasted_iota(jnp.int32, (chunk, LANES), 1)

        def body(s, sc):
            sel = lax.broadcasted_iota(jnp.int32, (chunk, 1), 0) == s
            bs = jnp.sum(jnp.where(sel, b, 0.0), axis=0, keepdims=True)
            ks = jnp.sum(jnp.where(sel, k, 0.0), axis=0, keepdims=True)
            col = jnp.sum(q * ks * jnp.exp(jnp.minimum(b - bs, 0.0)), axis=-1, keepdims=True)
            return jnp.where(lane == s, col, sc)

        sc = lax.fori_loop(0, chunk, body, jnp.zeros((chunk, LANES), F32))
        return sc[:, :chunk]

    def head_cols(h, sec):
        return slice(sec * HG_WIDTH + h * HG_HEAD_DIM, sec * HG_WIDTH + (h + 1) * HG_HEAD_DIM)

    def finish_head(rows, h, st, qe, sc, kd, decay):
        lanes = slice(h * HG_HEAD_DIM, (h + 1) * HG_HEAD_DIM)
        v = hg_ref[rows, head_cols(h, 3)].astype(BF16)
        gate = hg_ref[rows, head_cols(h, 4)]
        inter = lax.dot_general(qe, st.astype(BF16), (((1,), (1,)), ((), ())),
                                preferred_element_type=F32)
        sc = jnp.where(causal, sc, 0.0).astype(BF16)
        o = inter + jnp.dot(sc, v, preferred_element_type=F32)
        st_ref[h] = st * decay + lax.dot_general(v, kd, (((0,), (0,)), ((), ())),
                                                 preferred_element_type=F32)
        o = _rms(o, gn_ref[:, lanes]) * (gate * jax.nn.sigmoid(gate))
        o_ref[rows, lanes] = o.astype(o_ref.dtype)

    def cumulative_decay(rows):
        g_all = hg_ref[rows, 1 * HG_WIDTH:2 * HG_WIDTH]
        return sum(jnp.dot(tri, part, preferred_element_type=F32) for part in _split3(g_all))

    def decayed_operands(c):
        rows = slice(c * chunk, (c + 1) * chunk)
        b = cumulative_decay(rows)
        q = hg_ref[rows, 0:HG_WIDTH]
        k = hg_ref[rows, 2 * HG_WIDTH:3 * HG_WIDTH]
        ref = b[mid - 1:mid]
        last = b[chunk - 1:chunk]
        return ((q * jnp.exp(b)).astype(BF16), (q * jnp.exp(b - ref)).astype(BF16),
                (k * jnp.exp(ref - b)).astype(BF16), (k * jnp.exp(last - b)).astype(BF16),
                jnp.exp(last), jnp.maximum(b[0:1] - ref, ref - last))

    def run_fast():
        worst = jnp.zeros((1, HG_WIDTH), F32)
        ahead = decayed_operands(0)
        for c in range(n_chunks):
            qe, qh, kh, kd, decay, w = ahead
            if c + 1 < n_chunks:
                ahead = decayed_operands(c + 1)
            worst = jnp.maximum(worst, w)
            rows = slice(c * chunk, (c + 1) * chunk)
            for h in range(HG_HEADS):
                lanes = slice(h * HG_HEAD_DIM, (h + 1) * HG_HEAD_DIM)
                sc = lax.dot_general(qh[:, lanes], kh[:, lanes], (((1,), (1,)), ((), ())),
                                     preferred_element_type=F32)
                finish_head(rows, h, st_ref[h], qe[:, lanes], sc, kd[:, lanes], decay[:, lanes])
        return worst

    def run_exact():
        def chunk_step(c, carry):
            rows = pl.ds(pl.multiple_of(c * chunk, chunk), chunk)
            b_all = cumulative_decay(rows)
            for h in range(HG_HEADS):
                lanes = slice(h * HG_HEAD_DIM, (h + 1) * HG_HEAD_DIM)
                q = hg_ref[rows, head_cols(h, 0)]
                k = hg_ref[rows, head_cols(h, 2)]
                b = b_all[:, lanes]
                last = b[chunk - 1:chunk]
                finish_head(rows, h, st_ref[h], (q * jnp.exp(b)).astype(BF16),
                            direct_scores(q, k, b), (k * jnp.exp(last - b)).astype(BF16),
                            jnp.exp(last))
            return carry

        lax.fori_loop(0, n_chunks, chunk_step, 0)

    sstart_ref[...] = st_ref[...]
    worst = run_fast()

    @pl.when(jnp.max(worst) > SAFE_EXP)
    def _():
        st_ref[...] = sstart_ref[...]
        run_exact()

    @pl.when(t == pl.num_programs(1) - 1)
    def _():
        for h in range(HG_HEADS):
            sout_ref[0, h] = st_ref[h].T


def _hgrn(hg, s0, gn, batch, seq):
    chunk = min(CHUNK, seq)
    block = min(ROW_TILE, seq)
    n_blocks = seq // block
    state_spec = pl.BlockSpec((1, HG_HEADS, HG_HEAD_DIM, HG_HEAD_DIM), lambda b, t: (b, 0, 0, 0))
    return pl.pallas_call(
        functools.partial(_hgrn_body, chunk, block // chunk),
        grid=(batch, n_blocks),
        in_specs=[pl.BlockSpec((block, HG_COLS), lambda b, t: (b * n_blocks + t, 0)),
                  state_spec, _resident((1, HG_WIDTH))],
        out_specs=[pl.BlockSpec((block, HG_WIDTH), lambda b, t: (b * n_blocks + t, 0)),
                   state_spec],
        out_shape=[jax.ShapeDtypeStruct((batch * seq, HG_WIDTH), BF16),
                   jax.ShapeDtypeStruct((batch, HG_HEADS, HG_HEAD_DIM, HG_HEAD_DIM), F32)],
        scratch_shapes=[pltpu.VMEM((HG_HEADS, HG_HEAD_DIM, HG_HEAD_DIM), F32),
                        pltpu.VMEM((HG_HEADS, HG_HEAD_DIM, HG_HEAD_DIM), F32)],
        compiler_params=pltpu.CompilerParams(dimension_semantics=("parallel", "arbitrary"),
                                             vmem_limit_bytes=VMEM_LIMIT),
        name="hgrn_scan",
    )(hg, s0, gn)


def _softmax_pv(s, v_masked):
    m = jnp.max(s, axis=-1, keepdims=True)
    p = jnp.exp2(s - m)
    l = jnp.sum(p, axis=-1, keepdims=True)
    return jnp.dot(p.astype(BF16), v_masked, preferred_element_type=F32) / l


def _toeplitz(ext_row, n_q):
    ext = jnp.broadcast_to(ext_row, (n_q, EXT_LEN))
    return pltpu.roll(ext, EXT_LEN - n_q + 1, 1, stride=1, stride_axis=0)


def _reduce_rows(x, combine, finish, slab=CHUNK):
    acc = x[0:slab]
    for i in range(1, x.shape[0] // slab):
        acc = combine(acc, x[i * slab:(i + 1) * slab])
    return finish(acc, axis=0, keepdims=True)


def _attn_prompt_body(q_ref, k_ref, v_ref, ext_ref, gn_ref, o_ref, bias_ref):
    qb = pl.program_id(1)
    start = pl.multiple_of(qb * Q_BLOCK, Q_BLOCK)

    @pl.when(qb == 0)
    def _():
        q_chunk = lax.broadcasted_iota(jnp.int32, (Q_BLOCK, KEY_WINDOW), 0) // CHUNK
        k_chunk = lax.broadcasted_iota(jnp.int32, (Q_BLOCK, KEY_WINDOW), 1) // CHUNK
        band = (k_chunk >= q_chunk) & (k_chunk <= q_chunk + LEFT_CHUNKS)
        for h in range(ATT_HEADS):
            t = _toeplitz(ext_ref[h:h + 1, :], Q_BLOCK)[:, :KEY_WINDOW] * LOG2E
            bias_ref[h] = jnp.where(band, t, NEG).T

    def block(mask_prefix):
        lane = lax.broadcasted_iota(jnp.int32, (1, LANES), 1)
        if mask_prefix:
            group_row = lax.broadcasted_iota(jnp.int32, (GROUP_KEYS, LANES), 0)
        def pair_scores(pair):
            lanes = slice(pair * LANES, (pair + 1) * LANES)
            qp = q_ref[:, lanes]
            kp = k_ref[0, pl.ds(start, KEY_WINDOW), lanes]
            qq = jnp.concatenate([jnp.where((lane // ATT_HEAD_DIM) == sub, qp, 0)
                                  for sub in range(2)], axis=0)
            return lax.dot_general(kp, qq, (((1,), (1,)), ((), ())),
                                   preferred_element_type=F32)

        outs = []
        n_pairs = ATT_HEADS // 2
        s_next = pair_scores(0)
        for pair in range(n_pairs):
            lanes = slice(pair * LANES, (pair + 1) * LANES)
            s_pair = s_next
            if pair + 1 < n_pairs:
                s_next = pair_scores(pair + 1)
            vt = v_ref[0, pl.ds(start, KEY_WINDOW), lanes].T
            for sub in range(2):
                p_groups, l_groups = [], []
                for g in range(Q_BLOCK // LANES):
                    r0, r1 = g * LANES, g * LANES + GROUP_KEYS
                    qs = slice(g * LANES, (g + 1) * LANES)
                    s = (s_pair[r0:r1, sub * Q_BLOCK + g * LANES:sub * Q_BLOCK + (g + 1) * LANES]
                         + bias_ref[2 * pair + sub, r0:r1, qs])
                    if mask_prefix:
                        s = jnp.where(group_row + (r0 + start) >= ATT_REACH, s, NEG)
                    p = jnp.exp2(s - _reduce_rows(s, jnp.maximum, jnp.max))
                    l_groups.append(_reduce_rows(p, jnp.add, jnp.sum))
                    rows = [jnp.zeros((n, LANES), BF16) for n in (r0,) if n]
                    rows.append(p.astype(BF16))
                    rows += [jnp.zeros((n, LANES), BF16) for n in (KEY_WINDOW - r1,) if n]
                    p_groups.append(jnp.concatenate(rows, axis=0))
                o = jnp.dot(vt[sub * ATT_HEAD_DIM:(sub + 1) * ATT_HEAD_DIM],
                            jnp.concatenate(p_groups, axis=1),
                            preferred_element_type=F32)
                outs.append(jnp.concatenate(
                    [o[:, g * LANES:(g + 1) * LANES] / l for g, l in enumerate(l_groups)], axis=1))
        o = jnp.concatenate(outs, axis=0).T
        o_ref[...] = _rms(o, gn_ref[...]).astype(o_ref.dtype)

    n_prefix = ATT_REACH // Q_BLOCK

    @pl.when(qb < n_prefix)
    def _():
        block(True)

    @pl.when(qb >= n_prefix)
    def _():
        block(False)


def _attn_prompt(q, kpad, vpad, ext, gn, batch, seq):
    n_q = seq // Q_BLOCK
    kv_spec = pl.BlockSpec((1, seq + ATT_REACH, ATT_WIDTH), lambda b, i: (b, 0, 0))
    return pl.pallas_call(
        _attn_prompt_body,
        grid=(batch, n_q),
        in_specs=[pl.BlockSpec((Q_BLOCK, ATT_WIDTH), lambda b, i: (b * n_q + i, 0)),
                  kv_spec, kv_spec, _resident((ATT_HEADS, EXT_LEN)), _resident((1, ATT_WIDTH))],
        out_specs=pl.BlockSpec((Q_BLOCK, ATT_WIDTH), lambda b, i: (b * n_q + i, 0)),
        out_shape=jax.ShapeDtypeStruct((batch * seq, ATT_WIDTH), BF16),
        scratch_shapes=[pltpu.VMEM((ATT_HEADS, KEY_WINDOW, Q_BLOCK), F32)],
        compiler_params=pltpu.CompilerParams(dimension_semantics=("parallel", "arbitrary"),
                                             vmem_limit_bytes=VMEM_LIMIT),
        name="attn_prompt",
    )(q, kpad, vpad, ext, gn)


def _attn_sample_body(q_ref, k_ref, v_ref, ck_ref, cv_ref, ext_ref, gn_ref, o_ref):
    n_q = q_ref.shape[0]
    n_k = ck_ref.shape[2] + n_q
    lane = lax.broadcasted_iota(jnp.int32, (1, LANES), 1)
    outs = []
    for pair in range(ATT_HEADS // 2):
        lanes = slice(pair * LANES, (pair + 1) * LANES)
        qp = q_ref[:, lanes]
        kp = jnp.concatenate([ck_ref[0, 0, :, lanes].astype(BF16), k_ref[:, lanes]], axis=0)
        vp = jnp.concatenate([cv_ref[0, 0, :, lanes].astype(BF16), v_ref[:, lanes]], axis=0)
        acc = jnp.zeros((n_q, LANES), F32)
        for sub in range(2):
            h = 2 * pair + sub
            mine = (lane // ATT_HEAD_DIM) == sub
            s = lax.dot_general(jnp.where(mine, qp, 0), kp, (((1,), (1,)), ((), ())),
                                preferred_element_type=F32)
            bias = _toeplitz(ext_ref[h:h + 1, :], n_q)[:, :n_k] * LOG2E
            acc = acc + _softmax_pv(s + bias, jnp.where(mine, vp, 0))
        outs.append(acc)
    o = jnp.concatenate(outs, axis=-1)
    o_ref[...] = _rms(o, gn_ref[...]).astype(o_ref.dtype)


def _attn_sample(layer, q, k, v, cache_k, cache_v, ext, gn, batch, seq):
    past = cache_k.shape[2]
    row_spec = pl.BlockSpec((seq, ATT_WIDTH), lambda b: (b, 0))
    cache_spec = pl.BlockSpec((1, 1, past, ATT_WIDTH), lambda b: (layer, b, 0, 0))
    return pl.pallas_call(
        _attn_sample_body,
        grid=(batch,),
        in_specs=[row_spec, row_spec, row_spec, cache_spec, cache_spec,
                  _resident((ATT_HEADS, EXT_LEN)), _resident((1, ATT_WIDTH))],
        out_specs=row_spec,
        out_shape=jax.ShapeDtypeStruct((batch * seq, ATT_WIDTH), BF16),
        compiler_params=pltpu.CompilerParams(dimension_semantics=("parallel",),
                                             vmem_limit_bytes=VMEM_LIMIT),
        name="attn_sample",
    )(q, k, v, cache_k, cache_v, ext, gn)


def _mlp_body(final, x_ref, oh_ref, oa_ref, wo_ref, g2_ref, wu_ref, wd_ref, gf_ref, y_ref):
    mixed = jnp.concatenate([oh_ref[...], oa_ref[...]], axis=-1)
    x = x_ref[...] + jnp.dot(mixed, wo_ref[...], preferred_element_type=F32)
    h = _rms(x, g2_ref[...]).astype(BF16)
    mlp = None
    for c in range(D_FF // D_MODEL):
        cols = slice(c * D_MODEL, (c + 1) * D_MODEL)
        u = jnp.maximum(jnp.dot(h, wu_ref[:, cols], preferred_element_type=F32), 0.0)
        d = jnp.dot((u * u).astype(BF16), wd_ref[cols, :], preferred_element_type=F32)
        mlp = d if mlp is None else mlp + d
    x = x + mlp
    if final:
        x = _rms(x, gf_ref[...])
    y_ref[...] = x


def _mlp(final, x, oh, oa, wo, g2, wu, wd, gf):
    rows = x.shape[0]
    row_spec = lambda width: pl.BlockSpec((ROW_TILE, width), lambda i: (i, 0))
    return pl.pallas_call(
        functools.partial(_mlp_body, final),
        grid=(rows // ROW_TILE,),
        in_specs=[row_spec(D_MODEL), row_spec(HG_WIDTH), row_spec(ATT_WIDTH),
                  _resident((D_MODEL, D_MODEL)), _resident((1, D_MODEL)),
                  _resident((D_MODEL, D_FF)), _resident((D_FF, D_MODEL)), _resident((1, D_MODEL))],
        out_specs=row_spec(D_MODEL),
        out_shape=jax.ShapeDtypeStruct((rows, D_MODEL), F32),
        compiler_params=pltpu.CompilerParams(dimension_semantics=("parallel",),
                                             vmem_limit_bytes=VMEM_LIMIT),
        name="out_mlp",
    )(x, oh, oa, wo, g2, wu, wd, gf)


def _bias_by_distance(rel_bias_l, n_q, k_offset):
    left = n_q - 1 + k_offset - REL_CLIP
    right = EXT_LEN - (2 * REL_CLIP + 1) - left
    return jnp.pad(rel_bias_l.astype(F32), ((0, 0), (left, right)), mode="edge")


def kernel(x_prompt, x_sample, state_hgrn, cache_k, cache_v, lb_param, norm1_g, w_in, hg_norm_g,
           rel_bias, att_norm_g, w_out, norm2_g, w_up, w_down, final_norm_g):
    batch, seq, _ = x_prompt.shape
    dbatch, dseq, _ = x_sample.shape
    past = cache_k.shape[2]
    assert seq % ROW_TILE == 0 and min(ATT_REACH, seq) == ROW_TILE
    assert dbatch * dseq == ROW_TILE and dseq <= ATT_REACH
    xp = x_prompt.reshape(batch * seq, D_MODEL)
    xs = x_sample.reshape(dbatch * dseq, D_MODEL)
    ck = cache_k.reshape(DEPTH, dbatch, past, ATT_WIDTH)
    cv = cache_v.reshape(DEPTH, dbatch, past, ATT_WIDTH)
    zero_state = jnp.zeros((batch, HG_HEADS, HG_HEAD_DIM, HG_HEAD_DIM), F32)
    gf = final_norm_g.reshape(1, D_MODEL)
    row = lambda a, l: a[l].reshape(1, -1)
    pad = ((0, 0), (ATT_REACH, 0), (0, 0))

    sp_l, kp_l, vp_l, ss_l, ks_l, vs_l = [], [], [], [], [], []
    for l in range(DEPTH):
        w_in_l = w_in[l].astype(BF16)
        w_out_l = w_out[l].astype(BF16)
        w_up_l = w_up[l].astype(BF16)
        w_down_l = w_down[l].astype(BF16)
        g1, gh, ga, g2 = row(norm1_g, l), row(hg_norm_g, l), row(att_norm_g, l), row(norm2_g, l)
        final = l == DEPTH - 1

        hg, aq, ak, av, kf, vf = _inproj(l, xp, g1, w_in_l, lb_param, seq // ROW_TILE)
        oh, sp = _hgrn(hg, zero_state, gh, batch, seq)
        kpad = jnp.pad(ak.reshape(batch, seq, ATT_WIDTH), pad)
        vpad = jnp.pad(av.reshape(batch, seq, ATT_WIDTH), pad)
        ext_p = _bias_by_distance(rel_bias[l], Q_BLOCK, ATT_REACH)
        oa = _attn_prompt(aq, kpad, vpad, ext_p, ga, batch, seq)
        xp = _mlp(final, xp, oh, oa, w_out_l, g2, w_up_l, w_down_l, gf)
        sp_l.append(sp)
        kp_l.append(kf.reshape(batch, ROW_TILE, ATT_HEADS, ATT_HEAD_DIM))
        vp_l.append(vf.reshape(batch, ROW_TILE, ATT_HEADS, ATT_HEAD_DIM))

        hg, aq, ak, av, kf, vf = _inproj(l, xs, g1, w_in_l, lb_param, 1)
        oh, ss = _hgrn(hg, state_hgrn[l], gh, dbatch, dseq)
        ext_s = _bias_by_distance(rel_bias[l], dseq, past)
        oa = _attn_sample(l, aq, ak, av, ck, cv, ext_s, ga, dbatch, dseq)
        xs = _mlp(final, xs, oh, oa, w_out_l, g2, w_up_l, w_down_l, gf)
        ss_l.append(ss)
        ks_l.append(kf.reshape(dbatch, dseq, ATT_HEADS, ATT_HEAD_DIM))
        vs_l.append(vf.reshape(dbatch, dseq, ATT_HEADS, ATT_HEAD_DIM))

    return (xp.reshape(batch, seq, D_MODEL), xs.reshape(dbatch, dseq, D_MODEL),
            jnp.stack(sp_l), jnp.stack(kp_l), jnp.stack(vp_l),
            jnp.stack(ss_l), jnp.stack(ks_l), jnp.stack(vs_l))
```

```python
import functools
import math

import jax
import jax.numpy as jnp
from jax import lax
from jax.experimental import pallas as pl
from jax.experimental.pallas import tpu as pltpu

F32 = jnp.float32
BF16 = jnp.bfloat16

D_MODEL = 1024
DEPTH = 2
CHUNK = 64
HG_WIDTH = 512
HG_HEAD_DIM = 128
HG_HEADS = 4
ATT_WIDTH = 512
ATT_HEAD_DIM = 64
ATT_HEADS = 8
LEFT_CHUNKS = 8
ATT_REACH = LEFT_CHUNKS * CHUNK
REL_CLIP = 128
D_FF = 4 * D_MODEL
IN_WIDTH = 4 * HG_WIDTH + 3 * ATT_WIDTH
EPS = 1e-6
NEG = -1e30
LOG2E = math.log2(math.e)

LANES = 128
ROW_TILE = 512
Q_BLOCK = 256
KEY_WINDOW = ATT_REACH + Q_BLOCK
GROUP_KEYS = ATT_REACH + LANES
EXT_LEN = 1024
HG_COLS = 5 * HG_WIDTH
SAFE_EXP = 80.0
VMEM_LIMIT = 60 * 1024 * 1024


def _resident(shape):
    nd = len(shape)
    return pl.BlockSpec(shape, lambda *_: (0,) * nd, pipeline_mode=pl.Buffered(1))


def _rms(x, g):
    ms = jnp.mean(x * x, axis=-1, keepdims=True)
    return x * lax.rsqrt(ms + EPS) * g


def _inproj_body(layer, cache_period, x_ref, g_ref, w_ref, lbp_ref, hg_ref, aq_ref, ak_ref,
                 av_ref, kf_ref, vf_ref):
    h = _rms(x_ref[...], g_ref[...]).astype(BF16)

    def proj(sec):
        return jnp.dot(h, w_ref[:, sec * HG_WIDTH:(sec + 1) * HG_WIDTH],
                       preferred_element_type=F32)

    lp = lbp_ref[...]
    e = jnp.exp(lp - jnp.max(lp, axis=0, keepdims=True))
    sm = e / jnp.sum(e, axis=0, keepdims=True)
    lb = sm[0:1]
    for i in range(1, layer + 1):
        lb = lb + sm[i:i + 1]
    lb = lb - sm[0:1]

    f = lb + (1.0 - lb) * jax.nn.sigmoid(proj(1))
    hg_ref[:, 0 * HG_WIDTH:1 * HG_WIDTH] = proj(0)
    hg_ref[:, 1 * HG_WIDTH:2 * HG_WIDTH] = jnp.log(f)
    hg_ref[:, 2 * HG_WIDTH:3 * HG_WIDTH] = 1.0 - f
    hg_ref[:, 3 * HG_WIDTH:4 * HG_WIDTH] = proj(2)
    hg_ref[:, 4 * HG_WIDTH:5 * HG_WIDTH] = proj(3)
    aq_ref[...] = (proj(4) * (LOG2E / math.sqrt(ATT_HEAD_DIM))).astype(BF16)
    k = proj(5)
    v = proj(6)
    ak_ref[...] = k.astype(BF16)
    av_ref[...] = v.astype(BF16)

    @pl.when(pl.program_id(0) % cache_period == cache_period - 1)
    def _():
        kf_ref[...] = k
        vf_ref[...] = v


def _inproj(layer, x, g, w, lbp, cache_period):
    rows = x.shape[0]
    n_tiles = rows // ROW_TILE
    row_spec = lambda width: pl.BlockSpec((ROW_TILE, width), lambda i: (i, 0))
    cache_spec = pl.BlockSpec((ROW_TILE, ATT_WIDTH), lambda i: (i // cache_period, 0))
    cache_shape = jax.ShapeDtypeStruct((rows // cache_period, ATT_WIDTH), F32)
    return pl.pallas_call(
        functools.partial(_inproj_body, layer, cache_period),
        grid=(n_tiles,),
        in_specs=[row_spec(D_MODEL), _resident((1, D_MODEL)), _resident((D_MODEL, IN_WIDTH)),
                  _resident((DEPTH, HG_WIDTH))],
        out_specs=[row_spec(HG_COLS), row_spec(ATT_WIDTH), row_spec(ATT_WIDTH),
                   row_spec(ATT_WIDTH), cache_spec, cache_spec],
        out_shape=[jax.ShapeDtypeStruct((rows, HG_COLS), F32),
                   jax.ShapeDtypeStruct((rows, ATT_WIDTH), BF16),
                   jax.ShapeDtypeStruct((rows, ATT_WIDTH), BF16),
                   jax.ShapeDtypeStruct((rows, ATT_WIDTH), BF16),
                   cache_shape, cache_shape],
        compiler_params=pltpu.CompilerParams(dimension_semantics=("arbitrary",),
                                             vmem_limit_bytes=VMEM_LIMIT),
        name="inproj",
    )(x, g, w, lbp)


def _split3(x):
    hi = x.astype(BF16)
    r1 = x - hi.astype(F32)
    mid = r1.astype(BF16)
    lo = (r1 - mid.astype(F32)).astype(BF16)
    return hi, mid, lo


def _hgrn_body(chunk, n_chunks, hg_ref, s0_ref, gn_ref, o_ref, sout_ref, st_ref, sstart_ref):
    t = pl.program_id(1)
    mid = chunk // 2

    @pl.when(t == 0)
    def _():
        for h in range(HG_HEADS):
            st_ref[h] = s0_ref[0, h].T

    row_i = lax.broadcasted_iota(jnp.int32, (chunk, chunk), 0)
    col_i = lax.broadcasted_iota(jnp.int32, (chunk, chunk), 1)
    causal = row_i >= col_i
    tri = jnp.where(causal, 1.0, 0.0).astype(BF16)

    def direct_scores(q, k, b):
        lane = lax.broadcasted_iota(jnp.int32, (chunk, LANES), 1)

        def body(s, sc):
            sel = lax.broadcasted_iota(jnp.int32, (chunk, 1), 0) == s
            bs = jnp.sum(jnp.where(sel, b, 0.0), axis=0, keepdims=True)
            ks = jnp.sum(jnp.where(sel, k, 0.0), axis=0, keepdims=True)
            col = jnp.sum(q * ks * jnp.exp(jnp.minimum(b - bs, 0.0)), axis=-1, keepdims=True)
            return jnp.where(lane == s, col, sc)

        sc = lax.fori_loop(0, chunk, body, jnp.zeros((chunk, LANES), F32))
        return sc[:, :chunk]

    def head_cols(h, sec):
        return slice(sec * HG_WIDTH + h * HG_HEAD_DIM, sec * HG_WIDTH + (h + 1) * HG_HEAD_DIM)

    def finish_head(rows, h, st, qe, sc, kd, decay):
        lanes = slice(h * HG_HEAD_DIM, (h + 1) * HG_HEAD_DIM)
        v = hg_ref[rows, head_cols(h, 3)].astype(BF16)
        gate = hg_ref[rows, head_cols(h, 4)]
        inter = lax.dot_general(qe, st.astype(BF16), (((1,), (1,)), ((), ())),
                                preferred_element_type=F32)
        sc = jnp.where(causal, sc, 0.0).astype(BF16)
        o = inter + jnp.dot(sc, v, preferred_element_type=F32)
        st_ref[h] = st * decay + lax.dot_general(v, kd, (((0,), (0,)), ((), ())),
                                                 preferred_element_type=F32)
        o = _rms(o, gn_ref[:, lanes]) * (gate * jax.nn.sigmoid(gate))
        o_ref[rows, lanes] = o.astype(o_ref.dtype)

    def cumulative_decay(rows):
        g_all = hg_ref[rows, 1 * HG_WIDTH:2 * HG_WIDTH]
        return sum(jnp.dot(tri, part, preferred_element_type=F32) for part in _split3(g_all))

    def decayed_operands(c):
        rows = slice(c * chunk, (c + 1) * chunk)
        b = cumulative_decay(rows)
        q = hg_ref[rows, 0:HG_WIDTH]
        k = hg_ref[rows, 2 * HG_WIDTH:3 * HG_WIDTH]
        ref = b[mid - 1:mid]
        last = b[chunk - 1:chunk]
        return ((q * jnp.exp(b)).astype(BF16), (q * jnp.exp(b - ref)).astype(BF16),
                (k * jnp.exp(ref - b)).astype(BF16), (k * jnp.exp(last - b)).astype(BF16),
                jnp.exp(last), jnp.maximum(b[0:1] - ref, ref - last))

    def run_fast():
        worst = jnp.zeros((1, HG_WIDTH), F32)
        ahead = decayed_operands(0)
        for c in range(n_chunks):
            qe, qh, kh, kd, decay, w = ahead
            if c + 1 < n_chunks:
                ahead = decayed_operands(c + 1)
            worst = jnp.maximum(worst, w)
            rows = slice(c * chunk, (c + 1) * chunk)
            for h in range(HG_HEADS):
                lanes = slice(h * HG_HEAD_DIM, (h + 1) * HG_HEAD_DIM)
                sc = lax.dot_general(qh[:, lanes], kh[:, lanes], (((1,), (1,)), ((), ())),
                                     preferred_element_type=F32)
                finish_head(rows, h, st_ref[h], qe[:, lanes], sc, kd[:, lanes], decay[:, lanes])
        return worst

    def run_exact():
        def chunk_step(c, carry):
            rows = pl.ds(pl.multiple_of(c * chunk, chunk), chunk)
            b_all = cumulative_decay(rows)
            for h in range(HG_HEADS):
                lanes = slice(h * HG_HEAD_DIM, (h + 1) * HG_HEAD_DIM)
                q = hg_ref[rows, head_cols(h, 0)]
                k = hg_ref[rows, head_cols(h, 2)]
                b = b_all[:, lanes]
                last = b[chunk - 1:chunk]
                finish_head(rows, h, st_ref[h], (q * jnp.exp(b)).astype(BF16),
                            direct_scores(q, k, b), (k * jnp.exp(last - b)).astype(BF16),
                            jnp.exp(last))
            return carry

        lax.fori_loop(0, n_chunks, chunk_step, 0)

    sstart_ref[...] = st_ref[...]
    worst = run_fast()

    @pl.when(jnp.max(worst) > SAFE_EXP)
    def _():
        st_ref[...] = sstart_ref[...]
        run_exact()

    @pl.when(t == pl.num_programs(1) - 1)
    def _():
        for h in range(HG_HEADS):
            sout_ref[0, h] = st_ref[h].T


def _hgrn(hg, s0, gn, batch, seq):
    chunk = min(CHUNK, seq)
    block = min(ROW_TILE, seq)
    n_blocks = seq // block
    state_spec = pl.BlockSpec((1, HG_HEADS, HG_HEAD_DIM, HG_HEAD_DIM), lambda b, t: (b, 0, 0, 0))
    return pl.pallas_call(
        functools.partial(_hgrn_body, chunk, block // chunk),
        grid=(batch, n_blocks),
        in_specs=[pl.BlockSpec((block, HG_COLS), lambda b, t: (b * n_blocks + t, 0)),
                  state_spec, _resident((1, HG_WIDTH))],
        out_specs=[pl.BlockSpec((block, HG_WIDTH), lambda b, t: (b * n_blocks + t, 0)),
                   state_spec],
        out_shape=[jax.ShapeDtypeStruct((batch * seq, HG_WIDTH), BF16),
                   jax.ShapeDtypeStruct((batch, HG_HEADS, HG_HEAD_DIM, HG_HEAD_DIM), F32)],
        scratch_shapes=[pltpu.VMEM((HG_HEADS, HG_HEAD_DIM, HG_HEAD_DIM), F32),
                        pltpu.VMEM((HG_HEADS, HG_HEAD_DIM, HG_HEAD_DIM), F32)],
        compiler_params=pltpu.CompilerParams(dimension_semantics=("parallel", "arbitrary"),
                                             vmem_limit_bytes=VMEM_LIMIT),
        name="hgrn_scan",
    )(hg, s0, gn)


def _softmax_pv(s, v_masked):
    m = jnp.max(s, axis=-1, keepdims=True)
    p = jnp.exp2(s - m)
    l = jnp.sum(p, axis=-1, keepdims=True)
    return jnp.dot(p.astype(BF16), v_masked, preferred_element_type=F32) / l


def _toeplitz(ext_row, n_q):
    ext = jnp.broadcast_to(ext_row, (n_q, EXT_LEN))
    return pltpu.roll(ext, EXT_LEN - n_q + 1, 1, stride=1, stride_axis=0)


def _reduce_rows(x, combine, finish, slab=CHUNK):
    acc = x[0:slab]
    for i in range(1, x.shape[0] // slab):
        acc = combine(acc, x[i * slab:(i + 1) * slab])
    return finish(acc, axis=0, keepdims=True)


def _attn_prompt_body(q_ref, k_ref, v_ref, ext_ref, gn_ref, o_ref, bias_ref):
    qb = pl.program_id(1)
    n_prefix = ATT_REACH // Q_BLOCK

    @pl.when(qb == 0)
    def _():
        q_chunk = lax.broadcasted_iota(jnp.int32, (Q_BLOCK, KEY_WINDOW), 0) // CHUNK
        k_chunk = lax.broadcasted_iota(jnp.int32, (Q_BLOCK, KEY_WINDOW), 1) // CHUNK
        band = (k_chunk >= q_chunk) & (k_chunk <= q_chunk + LEFT_CHUNKS)
        for h in range(ATT_HEADS):
            t = _toeplitz(ext_ref[h:h + 1, :], Q_BLOCK)[:, :KEY_WINDOW] * LOG2E
            bias_ref[h] = jnp.where(band, t, NEG).T

    def block(prefix_block):
        lane = lax.broadcasted_iota(jnp.int32, (1, LANES), 1)
        mask_prefix = prefix_block is not None
        if mask_prefix:
            missing = ATT_REACH - prefix_block * Q_BLOCK
            group_row = lax.broadcasted_iota(jnp.int32, (GROUP_KEYS, LANES), 0)

            def window(ref, lanes):
                return jnp.concatenate([jnp.zeros((missing, LANES), BF16),
                                        ref[0, 0:KEY_WINDOW - missing, lanes]], axis=0)
        else:
            start = pl.multiple_of((qb - n_prefix) * Q_BLOCK, Q_BLOCK)

            def window(ref, lanes):
                return ref[0, pl.ds(start, KEY_WINDOW), lanes]
        def pair_scores(pair):
            lanes = slice(pair * LANES, (pair + 1) * LANES)
            qp = q_ref[:, lanes]
            kp = window(k_ref, lanes)
            qq = jnp.concatenate([jnp.where((lane // ATT_HEAD_DIM) == sub, qp, 0)
                                  for sub in range(2)], axis=0)
            return lax.dot_general(kp, qq, (((1,), (1,)), ((), ())),
                                   preferred_element_type=F32)

        outs = []
        n_pairs = ATT_HEADS // 2
        s_next = pair_scores(0)
        for pair in range(n_pairs):
            lanes = slice(pair * LANES, (pair + 1) * LANES)
            s_pair = s_next
            if pair + 1 < n_pairs:
                s_next = pair_scores(pair + 1)
            vt = window(v_ref, lanes).T
            for sub in range(2):
                p_groups, l_groups = [], []
                for g in range(Q_BLOCK // LANES):
                    r0, r1 = g * LANES, g * LANES + GROUP_KEYS
                    qs = slice(g * LANES, (g + 1) * LANES)
                    s = (s_pair[r0:r1, sub * Q_BLOCK + g * LANES:sub * Q_BLOCK + (g + 1) * LANES]
                         + bias_ref[2 * pair + sub, r0:r1, qs])
                    if mask_prefix:
                        s = jnp.where(group_row + r0 >= missing, s, NEG)
                    p = jnp.exp2(s - _reduce_rows(s, jnp.maximum, jnp.max))
                    l_groups.append(_reduce_rows(p, jnp.add, jnp.sum))
                    rows = [jnp.zeros((n, LANES), BF16) for n in (r0,) if n]
                    rows.append(p.astype(BF16))
                    rows += [jnp.zeros((n, LANES), BF16) for n in (KEY_WINDOW - r1,) if n]
                    p_groups.append(jnp.concatenate(rows, axis=0))
                o = jnp.dot(vt[sub * ATT_HEAD_DIM:(sub + 1) * ATT_HEAD_DIM],
                            jnp.concatenate(p_groups, axis=1),
                            preferred_element_type=F32)
                outs.append(jnp.concatenate(
                    [o[:, g * LANES:(g + 1) * LANES] / l for g, l in enumerate(l_groups)], axis=1))
        o = jnp.concatenate(outs, axis=0).T
        o_ref[...] = _rms(o, gn_ref[...]).astype(o_ref.dtype)

    for prefix_block in range(n_prefix):
        pl.when(qb == prefix_block)(functools.partial(block, prefix_block))

    @pl.when(qb >= n_prefix)
    def _():
        block(None)


def _attn_prompt(q, kpad, vpad, ext, gn, batch, seq):
    n_q = seq // Q_BLOCK
    kv_spec = pl.BlockSpec((1, seq, ATT_WIDTH), lambda b, i: (b, 0, 0))
    return pl.pallas_call(
        _attn_prompt_body,
        grid=(batch, n_q),
        in_specs=[pl.BlockSpec((Q_BLOCK, ATT_WIDTH), lambda b, i: (b * n_q + i, 0)),
                  kv_spec, kv_spec, _resident((ATT_HEADS, EXT_LEN)), _resident((1, ATT_WIDTH))],
        out_specs=pl.BlockSpec((Q_BLOCK, ATT_WIDTH), lambda b, i: (b * n_q + i, 0)),
        out_shape=jax.ShapeDtypeStruct((batch * seq, ATT_WIDTH), BF16),
        scratch_shapes=[pltpu.VMEM((ATT_HEADS, KEY_WINDOW, Q_BLOCK), F32)],
        compiler_params=pltpu.CompilerParams(dimension_semantics=("parallel", "arbitrary"),
                                             vmem_limit_bytes=VMEM_LIMIT),
        name="attn_prompt",
    )(q, kpad, vpad, ext, gn)


def _attn_sample_body(q_ref, k_ref, v_ref, ck_ref, cv_ref, ext_ref, gn_ref, o_ref):
    n_q = q_ref.shape[0]
    n_k = ck_ref.shape[2] + n_q
    lane = lax.broadcasted_iota(jnp.int32, (1, LANES), 1)
    outs = []
    for pair in range(ATT_HEADS // 2):
        lanes = slice(pair * LANES, (pair + 1) * LANES)
        qp = q_ref[:, lanes]
        kp = jnp.concatenate([ck_ref[0, 0, :, lanes].astype(BF16), k_ref[:, lanes]], axis=0)
        vp = jnp.concatenate([cv_ref[0, 0, :, lanes].astype(BF16), v_ref[:, lanes]], axis=0)
        acc = jnp.zeros((n_q, LANES), F32)
        for sub in range(2):
            h = 2 * pair + sub
            mine = (lane // ATT_HEAD_DIM) == sub
            s = lax.dot_general(jnp.where(mine, qp, 0), kp, (((1,), (1,)), ((), ())),
                                preferred_element_type=F32)
            bias = _toeplitz(ext_ref[h:h + 1, :], n_q)[:, :n_k] * LOG2E
            acc = acc + _softmax_pv(s + bias, jnp.where(mine, vp, 0))
        outs.append(acc)
    o = jnp.concatenate(outs, axis=-1)
    o_ref[...] = _rms(o, gn_ref[...]).astype(o_ref.dtype)


def _attn_sample(layer, q, k, v, cache_k, cache_v, ext, gn, batch, seq):
    past = cache_k.shape[2]
    row_spec = pl.BlockSpec((seq, ATT_WIDTH), lambda b: (b, 0))
    cache_spec = pl.BlockSpec((1, 1, past, ATT_WIDTH), lambda b: (layer, b, 0, 0))
    return pl.pallas_call(
        _attn_sample_body,
        grid=(batch,),
        in_specs=[row_spec, row_spec, row_spec, cache_spec, cache_spec,
                  _resident((ATT_HEADS, EXT_LEN)), _resident((1, ATT_WIDTH))],
        out_specs=row_spec,
        out_shape=jax.ShapeDtypeStruct((batch * seq, ATT_WIDTH), BF16),
        compiler_params=pltpu.CompilerParams(dimension_semantics=("parallel",),
                                             vmem_limit_bytes=VMEM_LIMIT),
        name="attn_sample",
    )(q, k, v, cache_k, cache_v, ext, gn)


def _mlp_body(final, x_ref, oh_ref, oa_ref, wo_ref, g2_ref, wu_ref, wd_ref, gf_ref, y_ref):
    mixed = jnp.concatenate([oh_ref[...], oa_ref[...]], axis=-1)
    x = x_ref[...] + jnp.dot(mixed, wo_ref[...], preferred_element_type=F32)
    h = _rms(x, g2_ref[...]).astype(BF16)
    mlp = None
    for c in range(D_FF // D_MODEL):
        cols = slice(c * D_MODEL, (c + 1) * D_MODEL)
        u = jnp.maximum(jnp.dot(h, wu_ref[:, cols], preferred_element_type=F32), 0.0)
        d = jnp.dot((u * u).astype(BF16), wd_ref[cols, :], preferred_element_type=F32)
        mlp = d if mlp is None else mlp + d
    x = x + mlp
    if final:
        x = _rms(x, gf_ref[...])
    y_ref[...] = x


def _mlp(final, x, oh, oa, wo, g2, wu, wd, gf):
    rows = x.shape[0]
    row_spec = lambda width: pl.BlockSpec((ROW_TILE, width), lambda i: (i, 0))
    return pl.pallas_call(
        functools.partial(_mlp_body, final),
        grid=(rows // ROW_TILE,),
        in_specs=[row_spec(D_MODEL), row_spec(HG_WIDTH), row_spec(ATT_WIDTH),
                  _resident((D_MODEL, D_MODEL)), _resident((1, D_MODEL)),
                  _resident((D_MODEL, D_FF)), _resident((D_FF, D_MODEL)), _resident((1, D_MODEL))],
        out_specs=row_spec(D_MODEL),
        out_shape=jax.ShapeDtypeStruct((rows, D_MODEL), F32),
        compiler_params=pltpu.CompilerParams(dimension_semantics=("parallel",),
                                             vmem_limit_bytes=VMEM_LIMIT),
        name="out_mlp",
    )(x, oh, oa, wo, g2, wu, wd, gf)


def _bias_by_distance(rel_bias_l, n_q, k_offset):
    left = n_q - 1 + k_offset - REL_CLIP
    right = EXT_LEN - (2 * REL_CLIP + 1) - left
    return jnp.pad(rel_bias_l.astype(F32), ((0, 0), (left, right)), mode="edge")


def kernel(x_prompt, x_sample, state_hgrn, cache_k, cache_v, lb_param, norm1_g, w_in, hg_norm_g,
           rel_bias, att_norm_g, w_out, norm2_g, w_up, w_down, final_norm_g):
    batch, seq, _ = x_prompt.shape
    dbatch, dseq, _ = x_sample.shape
    past = cache_k.shape[2]
    assert seq % ROW_TILE == 0 and min(ATT_REACH, seq) == ROW_TILE
    assert dbatch * dseq == ROW_TILE and dseq <= ATT_REACH
    xp = x_prompt.reshape(batch * seq, D_MODEL)
    xs = x_sample.reshape(dbatch * dseq, D_MODEL)
    ck = cache_k.reshape(DEPTH, dbatch, past, ATT_WIDTH)
    cv = cache_v.reshape(DEPTH, dbatch, past, ATT_WIDTH)
    zero_state = jnp.zeros((batch, HG_HEADS, HG_HEAD_DIM, HG_HEAD_DIM), F32)
    gf = final_norm_g.reshape(1, D_MODEL)
    row = lambda a, l: a[l].reshape(1, -1)

    sp_l, kp_l, vp_l, ss_l, ks_l, vs_l = [], [], [], [], [], []
    for l in range(DEPTH):
        w_in_l = w_in[l].astype(BF16)
        w_out_l = w_out[l].astype(BF16)
        w_up_l = w_up[l].astype(BF16)
        w_down_l = w_down[l].astype(BF16)
        g1, gh, ga, g2 = row(norm1_g, l), row(hg_norm_g, l), row(att_norm_g, l), row(norm2_g, l)
        final = l == DEPTH - 1

        hg, aq, ak, av, kf, vf = _inproj(l, xp, g1, w_in_l, lb_param, seq // ROW_TILE)
        oh, sp = _hgrn(hg, zero_state, gh, batch, seq)
        kpad = ak.reshape(batch, seq, ATT_WIDTH)
        vpad = av.reshape(batch, seq, ATT_WIDTH)
        ext_p = _bias_by_distance(rel_bias[l], Q_BLOCK, ATT_REACH)
        oa = _attn_prompt(aq, kpad, vpad, ext_p, ga, batch, seq)
        xp = _mlp(final, xp, oh, oa, w_out_l, g2, w_up_l, w_down_l, gf)
        sp_l.append(sp)
        kp_l.append(kf.reshape(batch, ROW_TILE, ATT_HEADS, ATT_HEAD_DIM))
        vp_l.append(vf.reshape(batch, ROW_TILE, ATT_HEADS, ATT_HEAD_DIM))

        hg, aq, ak, av, kf, vf = _inproj(l, xs, g1, w_in_l, lb_param, 1)
        oh, ss = _hgrn(hg, state_hgrn[l], gh, dbatch, dseq)
        ext_s = _bias_by_distance(rel_bias[l], dseq, past)
        oa = _attn_sample(l, aq, ak, av, ck, cv, ext_s, ga, dbatch, dseq)
        xs = _mlp(final, xs, oh, oa, w_out_l, g2, w_up_l, w_down_l, gf)
        ss_l.append(ss)
        ks_l.append(kf.reshape(dbatch, dseq, ATT_HEADS, ATT_HEAD_DIM))
        vs_l.append(vf.reshape(dbatch, dseq, ATT_HEADS, ATT_HEAD_DIM))

    return (xp.reshape(batch, seq, D_MODEL), xs.reshape(dbatch, dseq, D_MODEL),
            jnp.stack(sp_l), jnp.stack(kp_l), jnp.stack(vp_l),
            jnp.stack(ss_l), jnp.stack(ks_l), jnp.stack(vs_l))
```
